```python
import jax, jax.numpy as jnp
from jax import lax
import numpy as np

D_MODEL = 1024
BATCH = 8
SEQ = 8192
DEPTH = 4
DEC_BATCH = 32
DEC_SEQ = 64
PAST_LEN = 4096

CHUNK = 64
N_MIXERS = 2
N_RET_LAYERS = (DEPTH + 1) // 2
N_CONV_LAYERS = DEPTH // 2
RET_HEADS = 4
RET_DK = D_MODEL // RET_HEADS
RET_DV = 2 * RET_DK
RET_QKW = RET_HEADS * RET_DK
RET_VW = RET_HEADS * RET_DV
CONV_WIDTH = 3
D_FF = 4 * D_MODEL
PLE_DIM = 256
ROPE_BASE = 10000.0
NORM_EPS = 1e-6
GN_EPS = 1e-6

kernel_name = 'retnet_shortconv_hybrid_stream_step'


def rms_norm(x, g):
    xf = x.astype(jnp.float32)
    y = xf * lax.rsqrt(jnp.mean(xf * xf, axis=-1, keepdims=True) + NORM_EPS)
    return (y * g.astype(jnp.float32)).astype(x.dtype)


def log_gammas():
    return jnp.log(1.0 - 2.0 ** (-5.0 - jnp.arange(RET_HEADS, dtype=jnp.float32)))


def rotate_pairs(x, pos):
    inv = 1.0 / (ROPE_BASE ** jnp.linspace(0.0, 1.0, RET_DK // 2, dtype=jnp.float32))
    ang = pos[:, None] * inv[None, :]
    cos = jnp.cos(ang)[None, :, None, :]
    sin = jnp.sin(ang)[None, :, None, :]
    xr = x.reshape(x.shape[:-1] + (RET_DK // 2, 2))
    x0, x1 = xr[..., 0], xr[..., 1]
    out = jnp.stack([x0 * cos - x1 * sin, x0 * sin + x1 * cos], axis=-1)
    return out.reshape(x.shape)


def retention_scan(q, k, v, s0):
    b, l = q.shape[0], q.shape[1]
    c = min(l, CHUNK)
    n = l // c
    lg = log_gammas()
    idx = jnp.arange(c, dtype=jnp.float32)
    diff = idx[:, None] - idx[None, :]
    causal = diff >= 0
    intra = jnp.where(causal[None], jnp.exp(jnp.where(causal, diff, 0.0)[None] * lg[:, None, None]), 0.0)
    q_dec = jnp.exp((idx + 1.0)[:, None] * lg[None, :])[None, :, :, None]
    k_dec = jnp.exp((c - 1.0 - idx)[:, None] * lg[None, :])[None, :, :, None]
    s_dec = jnp.exp(c * lg)[None, :, None, None]

    def step(s, qkv):
        qc, kc, vc = qkv
        scores = jnp.einsum('bihd,bjhd->bhij', qc, kc) * intra[None]
        o = jnp.einsum('bhij,bjhe->bihe', scores, vc) + jnp.einsum('bihd,bhde->bihe', qc * q_dec, s)
        s = s * s_dec + jnp.einsum('bjhd,bjhe->bhde', kc * k_dec, vc)
        return s, o

    def to_chunks(t):
        return t.reshape((b, n, c) + t.shape[2:]).swapaxes(0, 1)

    s_fin, o = lax.scan(step, s0, (to_chunks(q), to_chunks(k), to_chunks(v)))
    return o.swapaxes(0, 1).reshape(b, l, RET_HEADS, RET_DV), s_fin


def retention_mixer(xn, s0, pos0, w_in, gn_g, gn_b, w_out):
    b, l, _ = xn.shape
    f32 = jnp.float32
    proj = xn @ w_in
    q, k, v, g = jnp.split(proj, [RET_QKW, 2 * RET_QKW, 2 * RET_QKW + RET_VW], axis=-1)
    pos = jnp.arange(l, dtype=f32) + pos0
    qf = rotate_pairs(q.astype(f32).reshape(b, l, RET_HEADS, RET_DK), pos)
    kf = rotate_pairs(k.astype(f32).reshape(b, l, RET_HEADS, RET_DK), pos) * (RET_DK ** -0.5)
    vf = v.astype(f32).reshape(b, l, RET_HEADS, RET_DV)
    o, s_fin = retention_scan(qf, kf, vf, s0.astype(f32))
    mu = jnp.mean(o, axis=-1, keepdims=True)
    var = jnp.mean(jnp.square(o - mu), axis=-1, keepdims=True)
    o = ((o - mu) * lax.rsqrt(var + GN_EPS)).reshape(b, l, RET_VW) * gn_g.astype(f32) + gn_b.astype(f32)
    y = (jax.nn.silu(g.astype(f32)) * o).astype(xn.dtype) @ w_out
    return y, s_fin


def conv_mixer(xn, hist, w_in, conv_w, w_out):
    l = xn.shape[1]
    bg, cg, h = jnp.split(xn @ w_in, 3, axis=-1)
    u = cg * h
    ext = jnp.concatenate([hist.astype(u.dtype), u], axis=1)
    y = conv_w[0] * ext[:, 0:l]
    for j in range(1, CONV_WIDTH):
        y = y + conv_w[j] * ext[:, j:j + l]
    return (bg * y) @ w_out, ext[:, -(CONV_WIDTH - 1):]


def trunk(x, p, ret_states, conv_states, pos0, g_mix, w_ret_in, ret_gn_g, ret_gn_b, w_ret_out,
          w_conv_in, conv_w, w_conv_out, g_mlp, w_up, w_down, g_ple, w_ple_gate, w_ple_proj, g_final):
    new_ret, new_conv = [], []
    h = x
    for i in range(DEPTH):
        xn = rms_norm(h, g_mix[i])
        r = i // N_MIXERS
        if i % N_MIXERS == 0:
            y, s = retention_mixer(xn, ret_states[r], pos0, w_ret_in[r], ret_gn_g[r], ret_gn_b[r], w_ret_out[r])
            new_ret.append(s.astype(x.dtype))
        else:
            y, s = conv_mixer(xn, conv_states[r], w_conv_in[r], conv_w[r], w_conv_out[r])
            new_conv.append(s.astype(x.dtype))
        h = h + y
        u = jax.nn.relu(rms_norm(h, g_mlp[i]) @ w_up[i])
        h = h + jnp.square(u) @ w_down[i]
        gate = jax.nn.sigmoid(rms_norm(h, g_ple[i]) @ w_ple_gate[i])
        h = h + (p[i] @ w_ple_proj[i]) * gate
    return rms_norm(h, g_final), jnp.stack(new_ret), jnp.stack(new_conv)


def setup_inputs(seed: int = 0) -> dict:
    key = jax.random.key(seed)
    ks = jax.random.split(key, 24)
    nrm = jax.random.normal
    f32 = jnp.float32
    d = D_MODEL
    return {
        'x_prompt': nrm(ks[0], (BATCH, SEQ, d), f32),
        'x_sample': nrm(ks[1], (DEC_BATCH, DEC_SEQ, d), f32),
        'state_ret': 0.5 * nrm(ks[2], (N_RET_LAYERS, DEC_BATCH, RET_HEADS, RET_DK, RET_DV), f32),
        'state_conv': nrm(ks[3], (N_CONV_LAYERS, DEC_BATCH, CONV_WIDTH - 1, d), f32),
        'p_prompt': nrm(ks[4], (DEPTH, BATCH, SEQ, PLE_DIM), f32),
        'p_sample': nrm(ks[5], (DEPTH, DEC_BATCH, DEC_SEQ, PLE_DIM), f32),
        'g_mix': 1.0 + 0.05 * nrm(ks[6], (DEPTH, d), f32),
        'w_ret_in': nrm(ks[7], (N_RET_LAYERS, d, 2 * RET_QKW + 2 * RET_VW), f32) * d ** -0.5,
        'ret_gn_g': 1.0 + 0.05 * nrm(ks[8], (N_RET_LAYERS, RET_VW), f32),
        'ret_gn_b': 0.02 * nrm(ks[9], (N_RET_LAYERS, RET_VW), f32),
        'w_ret_out': nrm(ks[10], (N_RET_LAYERS, RET_VW, d), f32) * RET_VW ** -0.5,
        'w_conv_in': nrm(ks[11], (N_CONV_LAYERS, d, 3 * d), f32) * d ** -0.5,
        'conv_w': nrm(ks[12], (N_CONV_LAYERS, CONV_WIDTH, d), f32) * CONV_WIDTH ** -0.5,
        'w_conv_out': nrm(ks[13], (N_CONV_LAYERS, d, d), f32) * d ** -0.5,
        'g_mlp': 1.0 + 0.05 * nrm(ks[14], (DEPTH, d), f32),
        'w_up': nrm(ks[15], (DEPTH, d, D_FF), f32) * d ** -0.5,
        'w_down': nrm(ks[16], (DEPTH, D_FF, d), f32) * D_FF ** -0.5,
        'g_ple': 1.0 + 0.05 * nrm(ks[17], (DEPTH, d), f32),
        'w_ple_gate': nrm(ks[18], (DEPTH, d, d), f32) * d ** -0.5,
        'w_ple_proj': nrm(ks[19], (DEPTH, PLE_DIM, d), f32) * PLE_DIM ** -0.5,
        'g_final': 1.0 + 0.05 * nrm(ks[20], (d,), f32),
    }


def reference(x_prompt, x_sample, state_ret, state_conv, p_prompt, p_sample, g_mix, w_ret_in, ret_gn_g,
              ret_gn_b, w_ret_out, w_conv_in, conv_w, w_conv_out, g_mlp, w_up, w_down, g_ple, w_ple_gate,
              w_ple_proj, g_final):
    b = x_prompt.shape[0]
    zero_ret = jnp.zeros((N_RET_LAYERS, b, RET_HEADS, RET_DK, RET_DV), x_prompt.dtype)
    zero_conv = jnp.zeros((N_CONV_LAYERS, b, CONV_WIDTH - 1, D_MODEL), x_prompt.dtype)
    y_prompt, ret_p, conv_p = trunk(x_prompt, p_prompt, zero_ret, zero_conv, 0, g_mix, w_ret_in, ret_gn_g,
                                    ret_gn_b, w_ret_out, w_conv_in, conv_w, w_conv_out, g_mlp, w_up, w_down,
                                    g_ple, w_ple_gate, w_ple_proj, g_final)
    y_sample, ret_s, conv_s = trunk(x_sample, p_sample, state_ret, state_conv, PAST_LEN, g_mix, w_ret_in,
                                    ret_gn_g, ret_gn_b, w_ret_out, w_conv_in, conv_w, w_conv_out, g_mlp, w_up,
                                    w_down, g_ple, w_ple_gate, w_ple_proj, g_final)
    return (y_prompt, y_sample, ret_p, conv_p, ret_s, conv_s)
```

```python
import functools

import numpy as np
import jax
import jax.numpy as jnp
from jax import lax
from jax.experimental import pallas as pl
from jax.experimental.pallas import tpu as pltpu

F32 = jnp.float32
BF16 = jnp.bfloat16

RET_HEADS = 4
CONV_WIDTH = 3
ROPE_BASE = 10000.0
NORM_EPS = 1e-6
GN_EPS = 1e-6
PAST_LEN = 4096

LANES = 128
SUBLANES = 8
VMEM_LIMIT_BYTES = 56 * 1024 * 1024

SEQ_TILE = 512
RET_CHUNK = 256
ROPE_TILE = 512


def _const_spec(shape):
    nd = len(shape)
    return pl.BlockSpec(shape, lambda *_: (0,) * nd, pipeline_mode=pl.Buffered(1))


def _rms(x, g):
    ms = jnp.mean(x * x, axis=-1, keepdims=True)
    return (x * lax.rsqrt(ms + NORM_EPS)) * g


def _dot(a, b):
    return jnp.dot(a, b, preferred_element_type=F32)


def _rope_table_kernel(inv_ref, cos_ref, sin_ref):
    rows = cos_ref.shape[0]
    pos = lax.broadcasted_iota(jnp.int32, cos_ref.shape, 0) + pl.program_id(0) * rows
    ang = pos.astype(F32) * inv_ref[...]
    cos_ref[...] = jnp.cos(ang)
    sin_ref[...] = jnp.sin(ang)


def _rope_tables(n_pos, half):
    inv = 1.0 / (ROPE_BASE ** jnp.linspace(0.0, 1.0, half, dtype=F32))
    out = jax.ShapeDtypeStruct((n_pos, half), F32)
    return pl.pallas_call(
        _rope_table_kernel,
        grid=(n_pos // ROPE_TILE,),
        in_specs=[pl.BlockSpec((1, half), lambda i: (0, 0))],
        out_specs=[pl.BlockSpec((ROPE_TILE, half), lambda i: (i, 0))] * 2,
        out_shape=[out, out],
        name="rope_tables",
    )(inv.reshape(1, half))


def _log_gamma(h):
    return float(np.log(1.0 - 2.0 ** (-5.0 - h)))


def _ret_kernel(*refs, nb, tt, chunk, has_state):
    if has_state:
        (x_ref, cos_ref, sin_ref, gmix_ref, wq_ref, wk_ref, wv_ref, wg_ref, gng_ref, gnb_ref, wo_ref,
         s0_ref, o_ref, sout_ref, s_scr, gated_scr) = refs
    else:
        (x_ref, cos_ref, sin_ref, gmix_ref, wq_ref, wk_ref, wv_ref, wg_ref, gng_ref, gnb_ref, wo_ref,
         o_ref, sout_ref, s_scr, gated_scr) = refs
        s0_ref = None
    d = x_ref.shape[-1]
    dk = wq_ref.shape[1] // RET_HEADS
    dv = wv_ref.shape[1] // RET_HEADS
    hk = dk // 2
    lpr = dv // LANES
    t = pl.program_id(1)

    @pl.when(t == 0)
    def _load_state():
        if has_state:
            for b in range(nb):
                for h in range(RET_HEADS):
                    for par in range(2):
                        for j in range(lpr):
                            s_scr[b, h, par * hk:(par + 1) * hk, j * LANES:(j + 1) * LANES] = (
                                s0_ref[b, h, pl.ds(par * lpr + j, hk, stride=2 * lpr), :])
        else:
            s_scr[...] = jnp.zeros_like(s_scr)

    x = x_ref[...].reshape(nb * tt, d)
    xn = _rms(x, gmix_ref[...]).astype(BF16)
    cos = cos_ref[...]
    sin = sin_ref[...]
    if nb > 1:
        cos = jnp.concatenate([cos] * nb, axis=0)
        sin = jnp.concatenate([sin] * nb, axis=0)

    def rotate(z):
        z0, z1 = z[:, :hk], z[:, hk:]
        return jnp.concatenate([z0 * cos - z1 * sin, z0 * sin + z1 * cos], axis=1)

    row_cc = lax.broadcasted_iota(jnp.int32, (chunk, chunk), 0)
    col_cc = lax.broadcasted_iota(jnp.int32, (chunk, chunk), 1)
    diff = row_cc - col_cc
    row_v = lax.broadcasted_iota(jnp.int32, (chunk, dv), 0).astype(F32)
    row_k = lax.broadcasted_iota(jnp.int32, (chunk, dk), 0).astype(F32)

    for h in range(RET_HEADS):
        lg = _log_gamma(h)
        intra = jnp.where(diff >= 0, jnp.exp(jnp.maximum(diff, 0).astype(F32) * lg), 0.0)
        q_dec = jnp.exp((row_v + 1.0) * lg)
        k_dec = jnp.exp((chunk - 1.0 - row_k) * lg)
        s_dec = float(np.exp(chunk * lg))

        q = rotate(_dot(xn, wq_ref[:, h * dk:(h + 1) * dk]))
        k = rotate(_dot(xn, wk_ref[:, h * dk:(h + 1) * dk])) * (dk ** -0.5)
        v = _dot(xn, wv_ref[:, h * dv:(h + 1) * dv]).astype(BF16)
        g = _dot(xn, wg_ref[:, h * dv:(h + 1) * dv])
        qb = q.astype(BF16)
        gn_g = gng_ref[:, h * dv:(h + 1) * dv]
        gn_b = gnb_ref[:, h * dv:(h + 1) * dv]

        for b in range(nb):
            s = s_scr[b, h]
            outs = []
            for c in range(tt // chunk):
                r0 = b * tt + c * chunk
                qc = qb[r0:r0 + chunk]
                kc = k[r0:r0 + chunk]
                vc = v[r0:r0 + chunk]
                scores = lax.dot_general(qc, kc.astype(BF16), (((1,), (1,)), ((), ())),
                                         preferred_element_type=F32) * intra
                o = _dot(scores.astype(BF16), vc) + q_dec * _dot(qc, s.astype(BF16))
                kd = (kc * k_dec).astype(BF16)
                s = s * s_dec + lax.dot_general(kd, vc, (((0,), (0,)), ((), ())),
                                                preferred_element_type=F32)
                outs.append(o)
            s_scr[b, h] = s
            o = outs[0] if len(outs) == 1 else jnp.concatenate(outs, axis=0)
            mu = jnp.mean(o, axis=-1, keepdims=True)
            oc = o - mu
            var = jnp.mean(oc * oc, axis=-1, keepdims=True)
            on = (oc * lax.rsqrt(var + GN_EPS)) * gn_g + gn_b
            gb = g[b * tt:(b + 1) * tt]
            gated = (gb * jax.nn.sigmoid(gb)) * on
            gated_scr[b * tt:(b + 1) * tt, h * dv:(h + 1) * dv] = gated.astype(BF16)

    y = _dot(gated_scr[...], wo_ref[...])
    o_ref[...] = (x + y).reshape(nb, tt, d)

    @pl.when(t == pl.num_programs(1) - 1)
    def _store_state():
        for b in range(nb):
            for h in range(RET_HEADS):
                for par in range(2):
                    for j in range(lpr):
                        sout_ref[b, h, pl.ds(par * lpr + j, hk, stride=2 * lpr), :] = (
                            s_scr[b, h, par * hk:(par + 1) * hk, j * LANES:(j + 1) * LANES])


def _ret_mixer(x, cos, sin, pos0, gmix, wq, wk, wv, wg, gn_g, gn_b, wo, s0, *, nb, tt):
    bsz, seq, d = x.shape
    qkw, vw = wq.shape[1], wv.shape[1]
    dk, dv = qkw // RET_HEADS, vw // RET_HEADS
    chunk = min(tt, RET_CHUNK)
    has_state = s0 is not None
    pos_blk = pos0 // tt
    in_specs = [
        pl.BlockSpec((nb, tt, d), lambda b, t: (b, t, 0)),
        pl.BlockSpec((tt, dk // 2), lambda b, t: (pos_blk + t, 0)),
        pl.BlockSpec((tt, dk // 2), lambda b, t: (pos_blk + t, 0)),
        _const_spec((1, d)),
        _const_spec((d, qkw)), _const_spec((d, qkw)), _const_spec((d, vw)), _const_spec((d, vw)),
        _const_spec((1, vw)), _const_spec((1, vw)),
        _const_spec((vw, d)),
    ]
    args = [x, cos, sin, gmix, wq, wk, wv, wg, gn_g, gn_b, wo]
    view = (bsz, RET_HEADS, dk * dv // LANES, LANES)
    state_spec = pl.BlockSpec((nb,) + view[1:], lambda b, t: (b, 0, 0, 0))
    if has_state:
        in_specs.append(state_spec)
        args.append(s0.reshape(view))
    h_out, s_out = pl.pallas_call(
        functools.partial(_ret_kernel, nb=nb, tt=tt, chunk=chunk, has_state=has_state),
        grid=(bsz // nb, seq // tt),
        in_specs=in_specs,
        out_specs=[pl.BlockSpec((nb, tt, d), lambda b, t: (b, t, 0)), state_spec],
        out_shape=[jax.ShapeDtypeStruct(x.shape, x.dtype),
                   jax.ShapeDtypeStruct(view, x.dtype)],
        scratch_shapes=[pltpu.VMEM((nb, RET_HEADS, dk, dv), F32),
                        pltpu.VMEM((nb * tt, vw), BF16)],
        compiler_params=pltpu.CompilerParams(dimension_semantics=("parallel", "arbitrary"),
                                             vmem_limit_bytes=VMEM_LIMIT_BYTES),
        name="ret_mixer_state" if has_state else "ret_mixer",
    )(*args)
    return h_out, s_out.reshape(bsz, RET_HEADS, dk, dv)


def _conv_kernel(*refs, nb, tt, has_state):
    if has_state:
        x_ref, gmix_ref, win_ref, cw_ref, wo_ref, h0_ref, o_ref, hout_ref, u_scr = refs
    else:
        x_ref, gmix_ref, win_ref, cw_ref, wo_ref, o_ref, hout_ref, u_scr = refs
        h0_ref = None
    d = x_ref.shape[-1]
    nh = CONV_WIDTH - 1
    head = SUBLANES
    t = pl.program_id(1)

    @pl.when(t == 0)
    def _load_hist():
        if has_state:
            u_scr[:, head - nh:head, :] = h0_ref[...]
        else:
            u_scr[:, head - nh:head, :] = jnp.zeros((nb, nh, d), F32)

    x = x_ref[...].reshape(nb * tt, d)
    xn = _rms(x, gmix_ref[...]).astype(BF16)
    bg = _dot(xn, win_ref[:, 0:d])
    u = _dot(xn, win_ref[:, d:2 * d]) * _dot(xn, win_ref[:, 2 * d:3 * d])
    u_scr[:, head:head + tt, :] = u.reshape(nb, tt, d)
    y = cw_ref[CONV_WIDTH - 1:CONV_WIDTH, :] * u
    for j in range(CONV_WIDTH - 1):
        off = head - nh + j
        y = y + cw_ref[j:j + 1, :] * u_scr[:, off:off + tt, :].reshape(nb * tt, d)
    z = _dot((bg * y).astype(BF16), wo_ref[...])
    o_ref[...] = (x + z).reshape(nb, tt, d)
    tail = u_scr[:, head + tt - nh:head + tt, :]
    u_scr[:, head - nh:head, :] = tail
    hout_ref[...] = tail


def _conv_mixer(x, gmix, w_in, conv_w, wo, h0, *, nb, tt):
    bsz, seq, d = x.shape
    nh = CONV_WIDTH - 1
    has_state = h0 is not None
    in_specs = [
        pl.BlockSpec((nb, tt, d), lambda b, t: (b, t, 0)),
        _const_spec((1, d)),
        _const_spec((d, 3 * d)),
        _const_spec((CONV_WIDTH, d)),
        _const_spec((d, d)),
    ]
    args = [x, gmix, w_in, conv_w, wo]
    hist_spec = pl.BlockSpec((nb, nh, d), lambda b, t: (b, 0, 0))
    if has_state:
        in_specs.append(hist_spec)
        args.append(h0)
    return pl.pallas_call(
        functools.partial(_conv_kernel, nb=nb, tt=tt, has_state=has_state),
        grid=(bsz // nb, seq // tt),
        in_specs=in_specs,
        out_specs=[pl.BlockSpec((nb, tt, d), lambda b, t: (b, t, 0)), hist_spec],
        out_shape=[jax.ShapeDtypeStruct(x.shape, x.dtype),
                   jax.ShapeDtypeStruct((bsz, nh, d), x.dtype)],
        scratch_shapes=[pltpu.VMEM((nb, SUBLANES + tt, d), F32)],
        compiler_params=pltpu.CompilerParams(dimension_semantics=("parallel", "arbitrary"),
                                             vmem_limit_bytes=VMEM_LIMIT_BYTES),
        name="conv_mixer_state" if has_state else "conv_mixer",
    )(*args)


def _ffn_kernel(x_ref, p_ref, gmlp_ref, wup_ref, wdn_ref, gple_ref, wgate_ref, wproj_ref, gfin_ref,
                o_ref, u_scr, *, final):
    d = x_ref.shape[-1]
    dff = wup_ref.shape[1]
    x = x_ref[...]
    xn = _rms(x, gmlp_ref[...]).astype(BF16)
    for j in range(dff // d):
        u = jnp.maximum(_dot(xn, wup_ref[:, j * d:(j + 1) * d]), 0.0)
        u_scr[:, j * d:(j + 1) * d] = (u * u).astype(BF16)
    h1 = x + _dot(u_scr[...], wdn_ref[...])
    gate = jax.nn.sigmoid(_dot(_rms(h1, gple_ref[...]).astype(BF16), wgate_ref[...]))
    h2 = h1 + _dot(p_ref[...].astype(BF16), wproj_ref[...]) * gate
    if final:
        h2 = _rms(h2, gfin_ref[...])
    o_ref[...] = h2


def _ffn_ple(x, p, gmlp, wup, wdn, gple, wgate, wproj, gfin, *, tile, final):
    n, d = x.shape
    dff = wup.shape[1]
    pd = p.shape[1]
    return pl.pallas_call(
        functools.partial(_ffn_kernel, final=final),
        grid=(n // tile,),
        in_specs=[
            pl.BlockSpec((tile, d), lambda i: (i, 0)),
            pl.BlockSpec((tile, pd), lambda i: (i, 0)),
            _const_spec((1, d)),
            _const_spec((d, dff)), _const_spec((dff, d)),
            _const_spec((1, d)),
            _const_spec((d, d)), _const_spec((pd, d)),
            _const_spec((1, d)),
        ],
        out_specs=pl.BlockSpec((tile, d), lambda i: (i, 0)),
        out_shape=jax.ShapeDtypeStruct(x.shape, x.dtype),
        scratch_shapes=[pltpu.VMEM((tile, dff), BF16)],
        compiler_params=pltpu.CompilerParams(dimension_semantics=("parallel",),
                                             vmem_limit_bytes=VMEM_LIMIT_BYTES),
        name="ffn_ple_final" if final else "ffn_ple",
    )(x, p, gmlp, wup, wdn, gple, wgate, wproj, gfin)


def _deinterleave_cols(w, dk):
    d, n = w.shape
    return w.reshape(d, n // dk, dk // 2, 2).swapaxes(2, 3).reshape(d, n)


def _trunk(x, p, ret_states, conv_states, pos0, cos, sin, wts, *, nb, tt, ffn_tile):
    bsz, seq, d = x.shape
    depth = p.shape[0]
    new_ret, new_conv = [], []
    h = x
    for i in range(depth):
        r = i // 2
        if i % 2 == 0:
            lw = wts["ret"][r]
            s0 = None if ret_states is None else ret_states[r]
            h, s = _ret_mixer(h, cos, sin, pos0, wts["g_mix"][i], lw["wq"], lw["wk"], lw["wv"], lw["wg"],
                              lw["gn_g"], lw["gn_b"], lw["wo"], s0, nb=nb, tt=tt)
            new_ret.append(s)
        else:
            lw = wts["conv"][r]
            h0 = None if conv_states is None else conv_states[r]
            h, s = _conv_mixer(h, wts["g_mix"][i], lw["w_in"], lw["conv_w"], lw["wo"], h0, nb=nb, tt=tt)
            new_conv.append(s)
        fw = wts["ffn"][i]
        h = _ffn_ple(h.reshape(bsz * seq, d), p[i].reshape(bsz * seq, -1), fw["g_mlp"], fw["w_up"],
                     fw["w_down"], fw["g_ple"], fw["w_gate"], fw["w_proj"], wts["g_final"],
                     tile=ffn_tile, final=(i == depth - 1)).reshape(bsz, seq, d)
    return h, jnp.stack(new_ret), jnp.stack(new_conv)


def kernel(x_prompt, x_sample, state_ret, state_conv, p_prompt, p_sample, g_mix, w_ret_in, ret_gn_g, ret_gn_b, w_ret_out, w_conv_in, conv_w, w_conv_out, g_mlp, w_up, w_down, g_ple, w_ple_gate, w_ple_proj, g_final):
    d = x_prompt.shape[-1]
    depth = p_prompt.shape[0]
    seq, dec_seq = x_prompt.shape[1], x_sample.shape[1]
    vw = w_ret_out.shape[1]
    qkw = (w_ret_in.shape[2] - 2 * vw) // 2
    dk = qkw // RET_HEADS

    row = lambda a: a.reshape(1, -1)
    wts = {
        "g_mix": [row(g_mix[i]) for i in range(depth)],
        "g_final": row(g_final),
        "ret": [], "conv": [], "ffn": [],
    }
    for r in range(w_ret_in.shape[0]):
        w = w_ret_in[r]
        wts["ret"].append({
            "wq": _deinterleave_cols(w[:, :qkw], dk).astype(BF16),
            "wk": _deinterleave_cols(w[:, qkw:2 * qkw], dk).astype(BF16),
            "wv": w[:, 2 * qkw:2 * qkw + vw].astype(BF16),
            "wg": w[:, 2 * qkw + vw:].astype(BF16),
            "gn_g": row(ret_gn_g[r]), "gn_b": row(ret_gn_b[r]),
            "wo": w_ret_out[r].astype(BF16),
        })
    for r in range(w_conv_in.shape[0]):
        wts["conv"].append({"w_in": w_conv_in[r].astype(BF16), "conv_w": conv_w[r],
                            "wo": w_conv_out[r].astype(BF16)})
    for i in range(depth):
        wts["ffn"].append({"g_mlp": row(g_mlp[i]), "w_up": w_up[i].astype(BF16),
                           "w_down": w_down[i].astype(BF16), "g_ple": row(g_ple[i]),
                           "w_gate": w_ple_gate[i].astype(BF16), "w_proj": w_ple_proj[i].astype(BF16)})

    n_pos = max(seq, PAST_LEN + dec_seq)
    n_pos = -(-n_pos // ROPE_TILE) * ROPE_TILE
    cos, sin = _rope_tables(n_pos, dk // 2)

    y_p, ret_p, conv_p = _trunk(x_prompt, p_prompt, None, None, 0, cos, sin, wts,
                                nb=1, tt=SEQ_TILE, ffn_tile=SEQ_TILE)
    y_s, ret_s, conv_s = _trunk(x_sample, p_sample, state_ret, state_conv, PAST_LEN, cos, sin, wts,
                                nb=2, tt=dec_seq, ffn_tile=SEQ_TILE)
    return (y_p, y_s, ret_p, conv_p, ret_s, conv_s)
```

```python
import functools

import numpy as np
import jax
import jax.numpy as jnp
from jax import lax
from jax.experimental import pallas as pl
from jax.experimental.pallas import tpu as pltpu

F32 = jnp.float32
BF16 = jnp.bfloat16

RET_HEADS = 4
CONV_WIDTH = 3
ROPE_BASE = 10000.0
NORM_EPS = 1e-6
GN_EPS = 1e-6
PAST_LEN = 4096

LANES = 128
SUBLANES = 8
VMEM_LIMIT_BYTES = 56 * 1024 * 1024

SEQ_TILE = 512
RET_CHUNK = 256
ROPE_TILE = 512


def _const_spec(shape):
    nd = len(shape)
    return pl.BlockSpec(shape, lambda *_: (0,) * nd, pipeline_mode=pl.Buffered(1))


def _rms(x, g):
    ms = jnp.mean(x * x, axis=-1, keepdims=True)
    return (x * lax.rsqrt(ms + NORM_EPS)) * g


def _dot(a, b):
    return jnp.dot(a, b, preferred_element_type=F32)


def _rope_table_kernel(inv_ref, sign_ref, cos_ref, sin_ref):
    rows = cos_ref.shape[0]
    pos = lax.broadcasted_iota(jnp.int32, cos_ref.shape, 0) + pl.program_id(0) * rows
    ang = pos.astype(F32) * inv_ref[...]
    cos_ref[...] = jnp.cos(ang)
    sin_ref[...] = jnp.sin(ang) * sign_ref[...]


def _rope_tables(n_pos, dk):
    inv = 1.0 / (ROPE_BASE ** jnp.linspace(0.0, 1.0, dk // 2, dtype=F32))
    inv = jnp.repeat(inv, 2).reshape(1, dk)
    sign = jnp.tile(jnp.array([-1.0, 1.0], F32), dk // 2).reshape(1, dk)
    out = jax.ShapeDtypeStruct((n_pos, dk), F32)
    return pl.pallas_call(
        _rope_table_kernel,
        grid=(n_pos // ROPE_TILE,),
        in_specs=[pl.BlockSpec((1, dk), lambda i: (0, 0))] * 2,
        out_specs=[pl.BlockSpec((ROPE_TILE, dk), lambda i: (i, 0))] * 2,
        out_shape=[out, out],
        name="rope_tables",
    )(inv, sign)


def _log_gamma(h):
    return float(np.log(1.0 - 2.0 ** (-5.0 - h)))


def _ret_kernel(*refs, nb, tt, chunk, has_state, has_prev):
    refs = list(refs)
    (x_ref, cos_ref, sin_ref, gmix_ref, wq_ref, wk_ref, wv_ref, wg_ref, gng_ref, gnb_ref, wo_ref) = refs[:11]
    del refs[:11]
    s0_ref = refs.pop(0) if has_state else None
    if has_prev:
        refs.pop(0)
    o_ref, sout_ref, s_scr, gated_scr = refs
    d = x_ref.shape[-1]
    dk = wq_ref.shape[1] // RET_HEADS
    dv = wv_ref.shape[1] // RET_HEADS
    t = pl.program_id(1)

    @pl.when(t == 0)
    def _load_state():
        if has_state:
            s_scr[...] = s0_ref[...]
        else:
            s_scr[...] = jnp.zeros_like(s_scr)

    x = x_ref[...].reshape(nb * tt, d)
    xn = _rms(x, gmix_ref[...]).astype(BF16)
    cos = cos_ref[...]
    sin = sin_ref[...]
    if nb > 1:
        cos = jnp.concatenate([cos] * nb, axis=0)
        sin = jnp.concatenate([sin] * nb, axis=0)
    even_lane = (lax.broadcasted_iota(jnp.int32, (nb * tt, LANES), 1) & 1) == 0

    def rotate(z):
        parts = []
        for j in range(dk // LANES):
            zj = z[:, j * LANES:(j + 1) * LANES]
            partner = jnp.where(even_lane, pltpu.roll(zj, LANES - 1, 1), pltpu.roll(zj, 1, 1))
            parts.append(zj * cos[:, j * LANES:(j + 1) * LANES] + partner * sin[:, j * LANES:(j + 1) * LANES])
        return jnp.concatenate(parts, axis=1)

    row_cc = lax.broadcasted_iota(jnp.int32, (chunk, chunk), 0)
    col_cc = lax.broadcasted_iota(jnp.int32, (chunk, chunk), 1)
    diff = row_cc - col_cc
    row_v = lax.broadcasted_iota(jnp.int32, (chunk, dv), 0).astype(F32)
    row_k = lax.broadcasted_iota(jnp.int32, (chunk, dk), 0).astype(F32)

    for h in range(RET_HEADS):
        lg = _log_gamma(h)
        intra = jnp.where(diff >= 0, jnp.exp(jnp.maximum(diff, 0).astype(F32) * lg), 0.0)
        q_dec = jnp.exp((row_v + 1.0) * lg)
        k_dec = jnp.exp((chunk - 1.0 - row_k) * lg)
        s_dec = float(np.exp(chunk * lg))

        q = rotate(_dot(xn, wq_ref[:, h * dk:(h + 1) * dk]))
        k = rotate(_dot(xn, wk_ref[:, h * dk:(h + 1) * dk])) * (dk ** -0.5)
        v = _dot(xn, wv_ref[:, h * dv:(h + 1) * dv]).astype(BF16)
        g = _dot(xn, wg_ref[:, h * dv:(h + 1) * dv])
        qb = q.astype(BF16)
        gn_g = gng_ref[:, h * dv:(h + 1) * dv]
        gn_b = gnb_ref[:, h * dv:(h + 1) * dv]

        for b in range(nb):
            s = s_scr[b, h]
            outs = []
            for c in range(tt // chunk):
                r0 = b * tt + c * chunk
                qc = qb[r0:r0 + chunk]
                kc = k[r0:r0 + chunk]
                vc = v[r0:r0 + chunk]
                scores = lax.dot_general(qc, kc.astype(BF16), (((1,), (1,)), ((), ())),
                                         preferred_element_type=F32) * intra
                o = _dot(scores.astype(BF16), vc) + q_dec * _dot(qc, s.astype(BF16))
                kd = (kc * k_dec).astype(BF16)
                s = s * s_dec + lax.dot_general(kd, vc, (((0,), (0,)), ((), ())),
                                                preferred_element_type=F32)
                outs.append(o)
            s_scr[b, h] = s
            o = outs[0] if len(outs) == 1 else jnp.concatenate(outs, axis=0)
            mu = jnp.mean(o, axis=-1, keepdims=True)
            oc = o - mu
            var = jnp.mean(oc * oc, axis=-1, keepdims=True)
            on = (oc * lax.rsqrt(var + GN_EPS)) * gn_g + gn_b
            gb = g[b * tt:(b + 1) * tt]
            gated = (gb * jax.nn.sigmoid(gb)) * on
            gated_scr[b * tt:(b + 1) * tt, h * dv:(h + 1) * dv] = gated.astype(BF16)

    y = _dot(gated_scr[...], wo_ref[...])
    o_ref[...] = (x + y).reshape(nb, tt, d)

    @pl.when(t == pl.num_programs(1) - 1)
    def _store_state():
        sout_ref[...] = s_scr[...]


def _ret_mixer(x, cos, sin, pos0, gmix, lw, states_in, states_out, layer, n_layers, *, nb, tt):
    bsz, seq, d = x.shape
    wq, wk, wv, wg, wo = lw["wq"], lw["wk"], lw["wv"], lw["wg"], lw["wo"]
    qkw, vw = wq.shape[1], wv.shape[1]
    dk, dv = qkw // RET_HEADS, vw // RET_HEADS
    chunk = min(tt, RET_CHUNK)
    assert pos0 % tt == 0 and seq % tt == 0 and tt % chunk == 0 and bsz % nb == 0
    has_state = states_in is not None
    has_prev = states_out is not None
    pos_blk = pos0 // tt
    in_specs = [
        pl.BlockSpec((nb, tt, d), lambda b, t: (b, t, 0)),
        pl.BlockSpec((tt, dk), lambda b, t: (pos_blk + t, 0)),
        pl.BlockSpec((tt, dk), lambda b, t: (pos_blk + t, 0)),
        _const_spec((1, d)),
        _const_spec((d, qkw)), _const_spec((d, qkw)), _const_spec((d, vw)), _const_spec((d, vw)),
        _const_spec((1, vw)), _const_spec((1, vw)),
        _const_spec((vw, d)),
    ]
    args = [x, cos, sin, gmix, wq, wk, wv, wg, lw["gn_g"], lw["gn_b"], wo]
    state_spec = pl.BlockSpec((None, nb, RET_HEADS, dk, dv), lambda b, t: (layer, b, 0, 0, 0))
    aliases = {}
    if has_state:
        in_specs.append(state_spec)
        args.append(states_in)
    if has_prev:
        aliases[len(args)] = 1
        in_specs.append(pl.BlockSpec(memory_space=pl.ANY))
        args.append(states_out)
    return pl.pallas_call(
        functools.partial(_ret_kernel, nb=nb, tt=tt, chunk=chunk, has_state=has_state, has_prev=has_prev),
        grid=(bsz // nb, seq // tt),
        in_specs=in_specs,
        out_specs=[pl.BlockSpec((nb, tt, d), lambda b, t: (b, t, 0)), state_spec],
        out_shape=[jax.ShapeDtypeStruct(x.shape, x.dtype),
                   jax.ShapeDtypeStruct((n_layers, bsz, RET_HEADS, dk, dv), x.dtype)],
        scratch_shapes=[pltpu.VMEM((nb, RET_HEADS, dk, dv), F32),
                        pltpu.VMEM((nb * tt, vw), BF16)],
        input_output_aliases=aliases,
        compiler_params=pltpu.CompilerParams(dimension_semantics=("parallel", "arbitrary"),
                                             vmem_limit_bytes=VMEM_LIMIT_BYTES),
        name="ret_mixer_state" if has_state else "ret_mixer",
    )(*args)


def _conv_kernel(*refs, nb, tt, has_state, has_prev):
    refs = list(refs)
    x_ref, gmix_ref, win_ref, cw_ref, wo_ref = refs[:5]
    del refs[:5]
    h0_ref = refs.pop(0) if has_state else None
    if has_prev:
        refs.pop(0)
    o_ref, hout_ref, u_scr = refs
    d = x_ref.shape[-1]
    nh = CONV_WIDTH - 1
    head = SUBLANES
    t = pl.program_id(1)

    @pl.when(t == 0)
    def _load_hist():
        if has_state:
            u_scr[:, head - nh:head, :] = h0_ref[...]
        else:
            u_scr[:, head - nh:head, :] = jnp.zeros((nb, nh, d), F32)

    x = x_ref[...].reshape(nb * tt, d)
    xn = _rms(x, gmix_ref[...]).astype(BF16)
    bg = _dot(xn, win_ref[:, 0:d])
    u = _dot(xn, win_ref[:, d:2 * d]) * _dot(xn, win_ref[:, 2 * d:3 * d])
    u_scr[:, head:head + tt, :] = u.reshape(nb, tt, d)
    y = cw_ref[CONV_WIDTH - 1:CONV_WIDTH, :] * u
    for j in range(CONV_WIDTH - 1):
        off = head - nh + j
        y = y + cw_ref[j:j + 1, :] * u_scr[:, off:off + tt, :].reshape(nb * tt, d)
    z = _dot((bg * y).astype(BF16), wo_ref[...])
    o_ref[...] = (x + z).reshape(nb, tt, d)
    tail = u_scr[:, head + tt - nh:head + tt, :]
    u_scr[:, head - nh:head, :] = tail
    hout_ref[...] = tail


def _conv_mixer(x, gmix, lw, hist_in, hist_out, layer, n_layers, *, nb, tt):
    bsz, seq, d = x.shape
    nh = CONV_WIDTH - 1
    assert seq % tt == 0 and bsz % nb == 0 and tt >= nh
    has_state = hist_in is not None
    has_prev = hist_out is not None
    in_specs = [
        pl.BlockSpec((nb, tt, d), lambda b, t: (b, t, 0)),
        _const_spec((1, d)),
        _const_spec((d, 3 * d)),
        _const_spec((CONV_WIDTH, d)),
        _const_spec((d, d)),
    ]
    args = [x, gmix, lw["w_in"], lw["conv_w"], lw["wo"]]
    hist_spec = pl.BlockSpec((None, nb, nh, d), lambda b, t: (layer, b, 0, 0))
    aliases = {}
    if has_state:
        in_specs.append(hist_spec)
        args.append(hist_in)
    if has_prev:
        aliases[len(args)] = 1
        in_specs.append(pl.BlockSpec(memory_space=pl.ANY))
        args.append(hist_out)
    return pl.pallas_call(
        functools.partial(_conv_kernel, nb=nb, tt=tt, has_state=has_state, has_prev=has_prev),
        grid=(bsz // nb, seq // tt),
        in_specs=in_specs,
        out_specs=[pl.BlockSpec((nb, tt, d), lambda b, t: (b, t, 0)), hist_spec],
        out_shape=[jax.ShapeDtypeStruct(x.shape, x.dtype),
                   jax.ShapeDtypeStruct((n_layers, bsz, nh, d), x.dtype)],
        scratch_shapes=[pltpu.VMEM((nb, SUBLANES + tt, d), F32)],
        input_output_aliases=aliases,
        compiler_params=pltpu.CompilerParams(dimension_semantics=("parallel", "arbitrary"),
                                             vmem_limit_bytes=VMEM_LIMIT_BYTES),
        name="conv_mixer_state" if has_state else "conv_mixer",
    )(*args)


def _ffn_kernel(x_ref, p_ref, gmlp_ref, wup_ref, wdn_ref, gple_ref, wgate_ref, wproj_ref, gfin_ref,
                o_ref, u_scr, *, final):
    d = x_ref.shape[-1]
    dff = wup_ref.shape[1]
    x = x_ref[...]
    xn = _rms(x, gmlp_ref[...]).astype(BF16)
    for j in range(dff // d):
        u = jnp.maximum(_dot(xn, wup_ref[:, j * d:(j + 1) * d]), 0.0)
        u_scr[:, j * d:(j + 1) * d] = (u * u).astype(BF16)
    h1 = x + _dot(u_scr[...], wdn_ref[...])
    gate = jax.nn.sigmoid(_dot(_rms(h1, gple_ref[...]).astype(BF16), wgate_ref[...]))
    h2 = h1 + _dot(p_ref[...].astype(BF16), wproj_ref[...]) * gate
    if final:
        h2 = _rms(h2, gfin_ref[...])
    o_ref[...] = h2


def _ffn_ple(x, p, layer, fw, gfin, *, tile, final):
    n, d = x.shape
    dff = fw["w_up"].shape[1]
    pd = p.shape[-1]
    tile = min(tile, n)
    assert n % tile == 0
    return pl.pallas_call(
        functools.partial(_ffn_kernel, final=final),
        grid=(n // tile,),
        in_specs=[
            pl.BlockSpec((tile, d), lambda i: (i, 0)),
            pl.BlockSpec((None, tile, pd), lambda i: (layer, i, 0)),
            _const_spec((1, d)),
            _const_spec((d, dff)), _const_spec((dff, d)),
            _const_spec((1, d)),
            _const_spec((d, d)), _const_spec((pd, d)),
            _const_spec((1, d)),
        ],
        out_specs=pl.BlockSpec((tile, d), lambda i: (i, 0)),
        out_shape=jax.ShapeDtypeStruct(x.shape, x.dtype),
        scratch_shapes=[pltpu.VMEM((tile, dff), BF16)],
        compiler_params=pltpu.CompilerParams(dimension_semantics=("parallel",),
                                             vmem_limit_bytes=VMEM_LIMIT_BYTES),
        name="ffn_ple_final" if final else "ffn_ple",
    )(x, p, fw["g_mlp"], fw["w_up"], fw["w_down"], fw["g_ple"], fw["w_gate"], fw["w_proj"], gfin)


def _trunk(x, p, ret_states, conv_states, pos0, cos, sin, wts, *, nb, tt, ffn_tile):
    bsz, seq, d = x.shape
    depth = p.shape[0]
    n_ret, n_conv = len(wts["ret"]), len(wts["conv"])
    p = p.reshape(depth, bsz * seq, p.shape[-1])
    new_ret = new_conv = None
    h = x
    for i in range(depth):
        r = i // 2
        if i % 2 == 0:
            h, new_ret = _ret_mixer(h, cos, sin, pos0, wts["g_mix"][i], wts["ret"][r], ret_states, new_ret,
                                    r, n_ret, nb=nb, tt=tt)
        else:
            h, new_conv = _conv_mixer(h, wts["g_mix"][i], wts["conv"][r], conv_states, new_conv,
                                      r, n_conv, nb=nb, tt=tt)
        h = _ffn_ple(h.reshape(bsz * seq, d), p, i, wts["ffn"][i], wts["g_final"],
                     tile=ffn_tile, final=(i == depth - 1)).reshape(bsz, seq, d)
    return h, new_ret, new_conv


def kernel(x_prompt, x_sample, state_ret, state_conv, p_prompt, p_sample, g_mix, w_ret_in, ret_gn_g, ret_gn_b, w_ret_out, w_conv_in, conv_w, w_conv_out, g_mlp, w_up, w_down, g_ple, w_ple_gate, w_ple_proj, g_final):
    depth = p_prompt.shape[0]
    seq, dec_seq = x_prompt.shape[1], x_sample.shape[1]
    vw = w_ret_out.shape[1]
    qkw = (w_ret_in.shape[2] - 2 * vw) // 2
    dk = qkw // RET_HEADS

    row = lambda a: a.reshape(1, -1)
    wts = {
        "g_mix": [row(g_mix[i]) for i in range(depth)],
        "g_final": row(g_final),
        "ret": [], "conv": [], "ffn": [],
    }
    for r in range(w_ret_in.shape[0]):
        w = w_ret_in[r]
        wts["ret"].append({
            "wq": w[:, :qkw].astype(BF16),
            "wk": w[:, qkw:2 * qkw].astype(BF16),
            "wv": w[:, 2 * qkw:2 * qkw + vw].astype(BF16),
            "wg": w[:, 2 * qkw + vw:].astype(BF16),
            "gn_g": row(ret_gn_g[r]), "gn_b": row(ret_gn_b[r]),
            "wo": w_ret_out[r].astype(BF16),
        })
    for r in range(w_conv_in.shape[0]):
        wts["conv"].append({"w_in": w_conv_in[r].astype(BF16), "conv_w": conv_w[r],
                            "wo": w_conv_out[r].astype(BF16)})
    for i in range(depth):
        wts["ffn"].append({"g_mlp": row(g_mlp[i]), "w_up": w_up[i].astype(BF16),
                           "w_down": w_down[i].astype(BF16), "g_ple": row(g_ple[i]),
                           "w_gate": w_ple_gate[i].astype(BF16), "w_proj": w_ple_proj[i].astype(BF16)})

    n_pos = max(seq, PAST_LEN + dec_seq)
    n_pos = -(-n_pos // ROPE_TILE) * ROPE_TILE
    cos, sin = _rope_tables(n_pos, dk)

    y_p, ret_p, conv_p = _trunk(x_prompt, p_prompt, None, None, 0, cos, sin, wts,
                                nb=1, tt=SEQ_TILE, ffn_tile=SEQ_TILE)
    y_s, ret_s, conv_s = _trunk(x_sample, p_sample, state_ret, state_conv, PAST_LEN, cos, sin, wts,
                                nb=2, tt=dec_seq, ffn_tile=SEQ_TILE)
    return (y_p, y_s, ret_p, conv_p, ret_s, conv_s)
```

```python
import functools

import numpy as np
import jax
import jax.numpy as jnp
from jax import lax
from jax.experimental import pallas as pl
from jax.experimental.pallas import tpu as pltpu

F32 = jnp.float32
BF16 = jnp.bfloat16

RET_HEADS = 4
CONV_WIDTH = 3
ROPE_BASE = 10000.0
NORM_EPS = 1e-6
GN_EPS = 1e-6
PAST_LEN = 4096

LANES = 128
SUBLANES = 8
VMEM_LIMIT_BYTES = 56 * 1024 * 1024

SEQ_TILE = 512
RET_CHUNK = 256
ROPE_TILE = 512


def _const_spec(shape):
    nd = len(shape)
    return pl.BlockSpec(shape, lambda *_: (0,) * nd, pipeline_mode=pl.Buffered(1))


def _rms(x, g):
    ms = jnp.mean(x * x, axis=-1, keepdims=True)
    return (x * lax.rsqrt(ms + NORM_EPS)) * g


def _dot(a, b):
    return jnp.dot(a, b, preferred_element_type=F32)


def _rope_table_kernel(inv_ref, sign_ref, cos_ref, sin_ref):
    rows = cos_ref.shape[0]
    pos = lax.broadcasted_iota(jnp.int32, cos_ref.shape, 0) + pl.program_id(0) * rows
    ang = pos.astype(F32) * inv_ref[...]
    cos_ref[...] = jnp.cos(ang)
    sin_ref[...] = jnp.sin(ang) * sign_ref[...]


def _rope_tables(n_pos, dk):
    inv = 1.0 / (ROPE_BASE ** jnp.linspace(0.0, 1.0, dk // 2, dtype=F32))
    inv = jnp.repeat(inv, 2).reshape(1, dk)
    sign = jnp.tile(jnp.array([-1.0, 1.0], F32), dk // 2).reshape(1, dk)
    out = jax.ShapeDtypeStruct((n_pos, dk), F32)
    return pl.pallas_call(
        _rope_table_kernel,
        grid=(n_pos // ROPE_TILE,),
        in_specs=[pl.BlockSpec((1, dk), lambda i: (0, 0))] * 2,
        out_specs=[pl.BlockSpec((ROPE_TILE, dk), lambda i: (i, 0))] * 2,
        out_shape=[out, out],
        name="rope_tables",
    )(inv, sign)


def _log_gamma(h):
    return float(np.log(1.0 - 2.0 ** (-5.0 - h)))


def _ret_kernel(*refs, nb, tt, chunk, has_state, has_prev):
    refs = list(refs)
    (x_ref, cos_ref, sin_ref, gmix_ref, wq_ref, wk_ref, wv_ref, wg_ref, gng_ref, gnb_ref, wo_ref) = refs[:11]
    del refs[:11]
    s0_ref = refs.pop(0) if has_state else None
    if has_prev:
        refs.pop(0)
    o_ref, sout_ref, s_scr, gated_scr = refs
    d = x_ref.shape[-1]
    dk = wq_ref.shape[1] // RET_HEADS
    dv = wv_ref.shape[1] // RET_HEADS
    t = pl.program_id(1)

    @pl.when(t == 0)
    def _load_state():
        if has_state:
            s_scr[...] = s0_ref[...]
        else:
            s_scr[...] = jnp.zeros_like(s_scr)

    x = x_ref[...].reshape(nb * tt, d)
    xn = _rms(x, gmix_ref[...]).astype(BF16)
    cos = cos_ref[...]
    sin = sin_ref[...]
    if nb > 1:
        cos = jnp.concatenate([cos] * nb, axis=0)
        sin = jnp.concatenate([sin] * nb, axis=0)
    even_lane = (lax.broadcasted_iota(jnp.int32, (nb * tt, LANES), 1) & 1) == 0
    groups = dk // LANES

    def rotate(z):
        parts = []
        for j in range(z.shape[1] // LANES):
            zj = z[:, j * LANES:(j + 1) * LANES]
            tj = j % groups
            partner = jnp.where(even_lane, pltpu.roll(zj, LANES - 1, 1), pltpu.roll(zj, 1, 1))
            parts.append(zj * cos[:, tj * LANES:(tj + 1) * LANES] + partner * sin[:, tj * LANES:(tj + 1) * LANES])
        return jnp.concatenate(parts, axis=1)

    def scale_rows(z, col):
        return jnp.concatenate([z[:, j * LANES:(j + 1) * LANES] * col for j in range(z.shape[1] // LANES)],
                               axis=1)

    q = rotate(_dot(xn, wq_ref[...])).astype(BF16)
    k = rotate(_dot(xn, wk_ref[...])) * (dk ** -0.5)
    v = _dot(xn, wv_ref[...]).astype(BF16)
    kb = k.astype(BF16)

    diff = (lax.broadcasted_iota(jnp.int32, (chunk, chunk), 0)
            - lax.broadcasted_iota(jnp.int32, (chunk, chunk), 1))
    row = lax.broadcasted_iota(jnp.int32, (chunk, LANES), 0).astype(F32)
    lgs = [_log_gamma(h) for h in range(RET_HEADS)]
    n_chunks = tt // chunk
    blocks = [(b, c) for b in range(nb) for c in range(n_chunks)]

    def rows(b, c):
        r0 = b * tt + c * chunk
        return slice(r0, r0 + chunk)

    scores = {}
    for (b, c) in blocks:
        for h in range(RET_HEADS):
            intra = jnp.where(diff >= 0, jnp.exp(jnp.maximum(diff, 0).astype(F32) * lgs[h]), 0.0)
            sc = lax.dot_general(q[rows(b, c), h * dk:(h + 1) * dk], kb[rows(b, c), h * dk:(h + 1) * dk],
                                 (((1,), (1,)), ((), ())), preferred_element_type=F32)
            scores[b, c, h] = (sc * intra).astype(BF16)

    outs = {}
    for (b, c) in blocks:
        for h in range(RET_HEADS):
            qc = q[rows(b, c), h * dk:(h + 1) * dk]
            vc = v[rows(b, c), h * dv:(h + 1) * dv]
            q_dec = jnp.exp((row + 1.0) * lgs[h])
            outs[b, c, h] = (_dot(scores[b, c, h], vc)
                             + scale_rows(_dot(qc, s_scr[b, h].astype(BF16)), q_dec))
        for h in range(RET_HEADS):
            k_dec = jnp.exp((chunk - 1.0 - row) * lgs[h])
            kd = scale_rows(k[rows(b, c), h * dk:(h + 1) * dk], k_dec).astype(BF16)
            vc = v[rows(b, c), h * dv:(h + 1) * dv]
            s_scr[b, h] = (s_scr[b, h] * float(np.exp(chunk * lgs[h]))
                           + lax.dot_general(kd, vc, (((0,), (0,)), ((), ())), preferred_element_type=F32))

    for h in range(RET_HEADS):
        g = _dot(xn, wg_ref[:, h * dv:(h + 1) * dv])
        gn_g = gng_ref[:, h * dv:(h + 1) * dv]
        gn_b = gnb_ref[:, h * dv:(h + 1) * dv]
        for b in range(nb):
            chunks = [outs[b, c, h] for c in range(n_chunks)]
            o = chunks[0] if n_chunks == 1 else jnp.concatenate(chunks, axis=0)
            mu = jnp.mean(o, axis=-1, keepdims=True)
            oc = o - mu
            var = jnp.mean(oc * oc, axis=-1, keepdims=True)
            on = (oc * lax.rsqrt(var + GN_EPS)) * gn_g + gn_b
            gb = g[b * tt:(b + 1) * tt]
            gated = (gb * jax.nn.sigmoid(gb)) * on
            gated_scr[b * tt:(b + 1) * tt, h * dv:(h + 1) * dv] = gated.astype(BF16)

    y = _dot(gated_scr[...], wo_ref[...])
    o_ref[...] = (x + y).reshape(nb, tt, d)

    @pl.when(t == pl.num_programs(1) - 1)
    def _store_state():
        sout_ref[...] = s_scr[...]


def _ret_mixer(x, cos, sin, pos0, gmix, lw, states_in, states_out, layer, n_layers, *, nb, tt):
    bsz, seq, d = x.shape
    wq, wk, wv, wg, wo = lw["wq"], lw["wk"], lw["wv"], lw["wg"], lw["wo"]
    qkw, vw = wq.shape[1], wv.shape[1]
    dk, dv = qkw // RET_HEADS, vw // RET_HEADS
    chunk = min(tt, RET_CHUNK)
    assert pos0 % tt == 0 and seq % tt == 0 and tt % chunk == 0 and bsz % nb == 0
    has_state = states_in is not None
    has_prev = states_out is not None
    pos_blk = pos0 // tt
    in_specs = [
        pl.BlockSpec((nb, tt, d), lambda b, t: (b, t, 0)),
        pl.BlockSpec((tt, dk), lambda b, t: (pos_blk + t, 0)),
        pl.BlockSpec((tt, dk), lambda b, t: (pos_blk + t, 0)),
        _const_spec((1, d)),
        _const_spec((d, qkw)), _const_spec((d, qkw)), _const_spec((d, vw)), _const_spec((d, vw)),
        _const_spec((1, vw)), _const_spec((1, vw)),
        _const_spec((vw, d)),
    ]
    args = [x, cos, sin, gmix, wq, wk, wv, wg, lw["gn_g"], lw["gn_b"], wo]
    state_spec = pl.BlockSpec((None, nb, RET_HEADS, dk, dv), lambda b, t: (layer, b, 0, 0, 0))
    aliases = {}
    if has_state:
        in_specs.append(state_spec)
        args.append(states_in)
    if has_prev:
        aliases[len(args)] = 1
        in_specs.append(pl.BlockSpec(memory_space=pl.ANY))
        args.append(states_out)
    return pl.pallas_call(
        functools.partial(_ret_kernel, nb=nb, tt=tt, chunk=chunk, has_state=has_state, has_prev=has_prev),
        grid=(bsz // nb, seq // tt),
        in_specs=in_specs,
        out_specs=[pl.BlockSpec((nb, tt, d), lambda b, t: (b, t, 0)), state_spec],
        out_shape=[jax.ShapeDtypeStruct(x.shape, x.dtype),
                   jax.ShapeDtypeStruct((n_layers, bsz, RET_HEADS, dk, dv), x.dtype)],
        scratch_shapes=[pltpu.VMEM((nb, RET_HEADS, dk, dv), F32),
                        pltpu.VMEM((nb * tt, vw), BF16)],
        input_output_aliases=aliases,
        compiler_params=pltpu.CompilerParams(dimension_semantics=("parallel", "arbitrary"),
                                             vmem_limit_bytes=VMEM_LIMIT_BYTES),
        name="ret_mixer_state" if has_state else "ret_mixer",
    )(*args)


def _conv_kernel(*refs, nb, tt, has_state, has_prev):
    refs = list(refs)
    x_ref, gmix_ref, win_ref, cw_ref, wo_ref = refs[:5]
    del refs[:5]
    h0_ref = refs.pop(0) if has_state else None
    if has_prev:
        refs.pop(0)
    o_ref, hout_ref, u_scr = refs
    d = x_ref.shape[-1]
    nh = CONV_WIDTH - 1
    head = SUBLANES
    t = pl.program_id(1)

    @pl.when(t == 0)
    def _load_hist():
        if has_state:
            u_scr[:, head - nh:head, :] = h0_ref[...]
        else:
            u_scr[:, head - nh:head, :] = jnp.zeros((nb, nh, d), F32)

    x = x_ref[...].reshape(nb * tt, d)
    xn = _rms(x, gmix_ref[...]).astype(BF16)
    bg = _dot(xn, win_ref[:, 0:d])
    u = _dot(xn, win_ref[:, d:2 * d]) * _dot(xn, win_ref[:, 2 * d:3 * d])
    u_scr[:, head:head + tt, :] = u.reshape(nb, tt, d)
    y = cw_ref[CONV_WIDTH - 1:CONV_WIDTH, :] * u
    for j in range(CONV_WIDTH - 1):
        off = head - nh + j
        y = y + cw_ref[j:j + 1, :] * u_scr[:, off:off + tt, :].reshape(nb * tt, d)
    z = _dot((bg * y).astype(BF16), wo_ref[...])
    o_ref[...] = (x + z).reshape(nb, tt, d)
    tail = u_scr[:, head + tt - nh:head + tt, :]
    u_scr[:, head - nh:head, :] = tail
    hout_ref[...] = tail


def _conv_mixer(x, gmix, lw, hist_in, hist_out, layer, n_layers, *, nb, tt):
    bsz, seq, d = x.shape
    nh = CONV_WIDTH - 1
    assert seq % tt == 0 and bsz % nb == 0 and tt >= nh
    has_state = hist_in is not None
    has_prev = hist_out is not None
    in_specs = [
        pl.BlockSpec((nb, tt, d), lambda b, t: (b, t, 0)),
        _const_spec((1, d)),
        _const_spec((d, 3 * d)),
        _const_spec((CONV_WIDTH, d)),
        _const_spec((d, d)),
    ]
    args = [x, gmix, lw["w_in"], lw["conv_w"], lw["wo"]]
    hist_spec = pl.BlockSpec((None, nb, nh, d), lambda b, t: (layer, b, 0, 0))
    aliases = {}
    if has_state:
        in_specs.append(hist_spec)
        args.append(hist_in)
    if has_prev:
        aliases[len(args)] = 1
        in_specs.append(pl.BlockSpec(memory_space=pl.ANY))
        args.append(hist_out)
    return pl.pallas_call(
        functools.partial(_conv_kernel, nb=nb, tt=tt, has_state=has_state, has_prev=has_prev),
        grid=(bsz // nb, seq // tt),
        in_specs=in_specs,
        out_specs=[pl.BlockSpec((nb, tt, d), lambda b, t: (b, t, 0)), hist_spec],
        out_shape=[jax.ShapeDtypeStruct(x.shape, x.dtype),
                   jax.ShapeDtypeStruct((n_layers, bsz, nh, d), x.dtype)],
        scratch_shapes=[pltpu.VMEM((nb, SUBLANES + tt, d), F32)],
        input_output_aliases=aliases,
        compiler_params=pltpu.CompilerParams(dimension_semantics=("parallel", "arbitrary"),
                                             vmem_limit_bytes=VMEM_LIMIT_BYTES),
        name="conv_mixer_state" if has_state else "conv_mixer",
    )(*args)


def _ffn_kernel(x_ref, p_ref, gmlp_ref, wup_ref, wdn_ref, gple_ref, wgate_ref, wproj_ref, gfin_ref,
                o_ref, u_scr, *, final):
    d = x_ref.shape[-1]
    dff = wup_ref.shape[1]
    x = x_ref[...]
    xn = _rms(x, gmlp_ref[...]).astype(BF16)
    for j in range(dff // d):
        u = jnp.maximum(_dot(xn, wup_ref[:, j * d:(j + 1) * d]), 0.0)
        u_scr[:, j * d:(j + 1) * d] = (u * u).astype(BF16)
    h1 = x + _dot(u_scr[...], wdn_ref[...])
    gate = jax.nn.sigmoid(_dot(_rms(h1, gple_ref[...]).astype(BF16), wgate_ref[...]))
    h2 = h1 + _dot(p_ref[...].astype(BF16), wproj_ref[...]) * gate
    if final:
        h2 = _rms(h2, gfin_ref[...])
    o_ref[...] = h2


def _ffn_ple(x, p, layer, fw, gfin, *, tile, final):
    n, d = x.shape
    dff = fw["w_up"].shape[1]
    pd = p.shape[-1]
    tile = min(tile, n)
    assert n % tile == 0
    return pl.pallas_call(
        functools.partial(_ffn_kernel, final=final),
        grid=(n // tile,),
        in_specs=[
            pl.BlockSpec((tile, d), lambda i: (i, 0)),
            pl.BlockSpec((None, tile, pd), lambda i: (layer, i, 0)),
            _const_spec((1, d)),
            _const_spec((d, dff)), _const_spec((dff, d)),
            _const_spec((1, d)),
            _const_spec((d, d)), _const_spec((pd, d)),
            _const_spec((1, d)),
        ],
        out_specs=pl.BlockSpec((tile, d), lambda i: (i, 0)),
        out_shape=jax.ShapeDtypeStruct(x.shape, x.dtype),
        scratch_shapes=[pltpu.VMEM((tile, dff), BF16)],
        compiler_params=pltpu.CompilerParams(dimension_semantics=("parallel",),
                                             vmem_limit_bytes=VMEM_LIMIT_BYTES),
        name="ffn_ple_final" if final else "ffn_ple",
    )(x, p, fw["g_mlp"], fw["w_up"], fw["w_down"], fw["g_ple"], fw["w_gate"], fw["w_proj"], gfin)


def _trunk(x, p, ret_states, conv_states, pos0, cos, sin, wts, *, nb, tt, ffn_tile):
    bsz, seq, d = x.shape
    depth = p.shape[0]
    n_ret, n_conv = len(wts["ret"]), len(wts["conv"])
    p = p.reshape(depth, bsz * seq, p.shape[-1])
    new_ret = new_conv = None
    h = x
    for i in range(depth):
        r = i // 2
        if i % 2 == 0:
            h, new_ret = _ret_mixer(h, cos, sin, pos0, wts["g_mix"][i], wts["ret"][r], ret_states, new_ret,
                                    r, n_ret, nb=nb, tt=tt)
        else:
            h, new_conv = _conv_mixer(h, wts["g_mix"][i], wts["conv"][r], conv_states, new_conv,
                                      r, n_conv, nb=nb, tt=tt)
        h = _ffn_ple(h.reshape(bsz * seq, d), p, i, wts["ffn"][i], wts["g_final"],
                     tile=ffn_tile, final=(i == depth - 1)).reshape(bsz, seq, d)
    return h, new_ret, new_conv


def kernel(x_prompt, x_sample, state_ret, state_conv, p_prompt, p_sample, g_mix, w_ret_in, ret_gn_g, ret_gn_b, w_ret_out, w_conv_in, conv_w, w_conv_out, g_mlp, w_up, w_down, g_ple, w_ple_gate, w_ple_proj, g_final):
    depth = p_prompt.shape[0]
    seq, dec_seq = x_prompt.shape[1], x_sample.shape[1]
    vw = w_ret_out.shape[1]
    qkw = (w_ret_in.shape[2] - 2 * vw) // 2
    dk = qkw // RET_HEADS

    row = lambda a: a.reshape(1, -1)
    wts = {
        "g_mix": [row(g_mix[i]) for i in range(depth)],
        "g_final": row(g_final),
        "ret": [], "conv": [], "ffn": [],
    }
    for r in range(w_ret_in.shape[0]):
        w = w_ret_in[r]
        wts["ret"].append({
            "wq": w[:, :qkw].astype(BF16),
            "wk": w[:, qkw:2 * qkw].astype(BF16),
            "wv": w[:, 2 * qkw:2 * qkw + vw].astype(BF16),
            "wg": w[:, 2 * qkw + vw:].astype(BF16),
            "gn_g": row(ret_gn_g[r]), "gn_b": row(ret_gn_b[r]),
            "wo": w_ret_out[r].astype(BF16),
        })
    for r in range(w_conv_in.shape[0]):
        wts["conv"].append({"w_in": w_conv_in[r].astype(BF16), "conv_w": conv_w[r],
                            "wo": w_conv_out[r].astype(BF16)})
    for i in range(depth):
        wts["ffn"].append({"g_mlp": row(g_mlp[i]), "w_up": w_up[i].astype(BF16),
                           "w_down": w_down[i].astype(BF16), "g_ple": row(g_ple[i]),
                           "w_gate": w_ple_gate[i].astype(BF16), "w_proj": w_ple_proj[i].astype(BF16)})

    n_pos = max(seq, PAST_LEN + dec_seq)
    n_pos = -(-n_pos // ROPE_TILE) * ROPE_TILE
    cos, sin = _rope_tables(n_pos, dk)

    y_p, ret_p, conv_p = _trunk(x_prompt, p_prompt, None, None, 0, cos, sin, wts,
                                nb=1, tt=SEQ_TILE, ffn_tile=SEQ_TILE)
    y_s, ret_s, conv_s = _trunk(x_sample, p_sample, state_ret, state_conv, PAST_LEN, cos, sin, wts,
                                nb=2, tt=dec_seq, ffn_tile=SEQ_TILE)
    return (y_p, y_s, ret_p, conv_p, ret_s, conv_s)
```

```python
import functools

import numpy as np
import jax
import jax.numpy as jnp
from jax import lax
from jax.experimental import pallas as pl
from jax.experimental.pallas import tpu as pltpu

F32 = jnp.float32
BF16 = jnp.bfloat16

RET_HEADS = 4
CONV_WIDTH = 3
ROPE_BASE = 10000.0
NORM_EPS = 1e-6
GN_EPS = 1e-6
PAST_LEN = 4096

LANES = 128
SUBLANES = 8
VMEM_LIMIT_BYTES = 56 * 1024 * 1024

SEQ_TILE = 512
CONV_TILE = 1024
FFN_TILE = 1024
FFN_TAIL_SPLIT = 4
RET_CHUNK = 256
ROPE_TILE = 512


def _const_spec(shape):
    nd = len(shape)
    return pl.BlockSpec(shape, lambda *_: (0,) * nd, pipeline_mode=pl.Buffered(1))


def _rms(x, g):
    ms = jnp.mean(x * x, axis=-1, keepdims=True)
    return (x * lax.rsqrt(ms + NORM_EPS)) * g


def _dot(a, b):
    return jnp.dot(a, b, preferred_element_type=F32)


def _rope_table_kernel(inv_ref, sign_ref, cos_ref, sin_ref):
    rows = cos_ref.shape[0]
    pos = lax.broadcasted_iota(jnp.int32, cos_ref.shape, 0) + pl.program_id(0) * rows
    ang = pos.astype(F32) * inv_ref[...]
    cos_ref[...] = jnp.cos(ang)
    sin_ref[...] = jnp.sin(ang) * sign_ref[...]


def _rope_tables(n_pos, dk):
    inv = 1.0 / (ROPE_BASE ** jnp.linspace(0.0, 1.0, dk // 2, dtype=F32))
    inv = jnp.repeat(inv, 2).reshape(1, dk)
    sign = jnp.tile(jnp.array([-1.0, 1.0], F32), dk // 2).reshape(1, dk)
    out = jax.ShapeDtypeStruct((n_pos, dk), F32)
    return pl.pallas_call(
        _rope_table_kernel,
        grid=(n_pos // ROPE_TILE,),
        in_specs=[pl.BlockSpec((1, dk), lambda i: (0, 0))] * 2,
        out_specs=[pl.BlockSpec((ROPE_TILE, dk), lambda i: (i, 0))] * 2,
        out_shape=[out, out],
        name="rope_tables",
    )(inv, sign)


def _log_gamma(h):
    return float(np.log(1.0 - 2.0 ** (-5.0 - h)))


def _decay_table_kernel(intra_ref, qdec_ref, kdec_ref):
    chunk = intra_ref.shape[1]
    diff = (lax.broadcasted_iota(jnp.int32, (chunk, chunk), 0)
            - lax.broadcasted_iota(jnp.int32, (chunk, chunk), 1))
    row = lax.broadcasted_iota(jnp.int32, (chunk, LANES), 0).astype(F32)
    for h in range(RET_HEADS):
        lg = _log_gamma(h)
        intra_ref[h] = jnp.where(diff >= 0, jnp.exp(jnp.maximum(diff, 0).astype(F32) * lg), 0.0)
        qdec_ref[h] = jnp.exp((row + 1.0) * lg)
        kdec_ref[h] = jnp.exp((chunk - 1.0 - row) * lg)


def _decay_tables(chunk):
    return pl.pallas_call(
        _decay_table_kernel,
        out_shape=[jax.ShapeDtypeStruct((RET_HEADS, chunk, chunk), F32),
                   jax.ShapeDtypeStruct((RET_HEADS, chunk, LANES), F32),
                   jax.ShapeDtypeStruct((RET_HEADS, chunk, LANES), F32)],
        name="decay_tables",
    )()


def _ret_kernel(*refs, nb, tt, chunk, has_state, has_prev):
    refs = list(refs)
    (x_ref, cos_ref, sin_ref, intra_ref, qdec_ref, kdec_ref, gmix_ref, wq_ref, wk_ref, wv_ref, wg_ref,
     gng_ref, gnb_ref, wo_ref) = refs[:14]
    del refs[:14]
    s0_ref = refs.pop(0) if has_state else None
    if has_prev:
        refs.pop(0)
    o_ref, sout_ref, s_scr, gated_scr = refs
    d = x_ref.shape[-1]
    dk = wq_ref.shape[1] // RET_HEADS
    dv = wv_ref.shape[1] // RET_HEADS
    t = pl.program_id(1)

    @pl.when(t == 0)
    def _load_state():
        if has_state:
            s_scr[...] = s0_ref[...]
        else:
            s_scr[...] = jnp.zeros_like(s_scr)

    x = x_ref[...].reshape(nb * tt, d)
    xn = _rms(x, gmix_ref[...]).astype(BF16)
    cos = cos_ref[...]
    sin = sin_ref[...]
    if nb > 1:
        cos = jnp.concatenate([cos] * nb, axis=0)
        sin = jnp.concatenate([sin] * nb, axis=0)
    even_lane = (lax.broadcasted_iota(jnp.int32, (nb * tt, LANES), 1) & 1) == 0
    groups = dk // LANES

    def rotate(z):
        parts = []
        for j in range(z.shape[1] // LANES):
            zj = z[:, j * LANES:(j + 1) * LANES]
            tj = j % groups
            partner = jnp.where(even_lane, pltpu.roll(zj, LANES - 1, 1), pltpu.roll(zj, 1, 1))
            parts.append(zj * cos[:, tj * LANES:(tj + 1) * LANES] + partner * sin[:, tj * LANES:(tj + 1) * LANES])
        return jnp.concatenate(parts, axis=1)

    def scale_rows(z, col):
        return jnp.concatenate([z[:, j * LANES:(j + 1) * LANES] * col for j in range(z.shape[1] // LANES)],
                               axis=1)

    q = rotate(_dot(xn, wq_ref[...])).astype(BF16)
    k = rotate(_dot(xn, wk_ref[...])) * (dk ** -0.5)
    v = _dot(xn, wv_ref[...]).astype(BF16)
    kb = k.astype(BF16)

    n_chunks = tt // chunk
    blocks = [(b, c) for b in range(nb) for c in range(n_chunks)]

    def rows(b, c):
        r0 = b * tt + c * chunk
        return slice(r0, r0 + chunk)

    scores = {}
    for (b, c) in blocks:
        for h in range(RET_HEADS):
            sc = lax.dot_general(q[rows(b, c), h * dk:(h + 1) * dk], kb[rows(b, c), h * dk:(h + 1) * dk],
                                 (((1,), (1,)), ((), ())), preferred_element_type=F32)
            scores[b, c, h] = (sc * intra_ref[h]).astype(BF16)

    outs = {}
    for (b, c) in blocks:
        for h in range(RET_HEADS):
            qc = q[rows(b, c), h * dk:(h + 1) * dk]
            vc = v[rows(b, c), h * dv:(h + 1) * dv]
            outs[b, c, h] = (_dot(scores[b, c, h], vc)
                             + scale_rows(_dot(qc, s_scr[b, h].astype(BF16)), qdec_ref[h]))
        for h in range(RET_HEADS):
            kd = scale_rows(k[rows(b, c), h * dk:(h + 1) * dk], kdec_ref[h]).astype(BF16)
            vc = v[rows(b, c), h * dv:(h + 1) * dv]
            s_scr[b, h] = (s_scr[b, h] * float(np.exp(chunk * _log_gamma(h)))
                           + lax.dot_general(kd, vc, (((0,), (0,)), ((), ())), preferred_element_type=F32))

    for h in range(RET_HEADS):
        g = _dot(xn, wg_ref[:, h * dv:(h + 1) * dv])
        gn_g = gng_ref[:, h * dv:(h + 1) * dv]
        gn_b = gnb_ref[:, h * dv:(h + 1) * dv]
        for b in range(nb):
            chunks = [outs[b, c, h] for c in range(n_chunks)]
            o = chunks[0] if n_chunks == 1 else jnp.concatenate(chunks, axis=0)
            mu = jnp.mean(o, axis=-1, keepdims=True)
            oc = o - mu
            var = jnp.mean(oc * oc, axis=-1, keepdims=True)
            on = (oc * lax.rsqrt(var + GN_EPS)) * gn_g + gn_b
            gb = g[b * tt:(b + 1) * tt]
            gated = (gb * jax.nn.sigmoid(gb)) * on
            gated_scr[b * tt:(b + 1) * tt, h * dv:(h + 1) * dv] = gated.astype(BF16)

    y = _dot(gated_scr[...], wo_ref[...])
    o_ref[...] = (x + y).reshape(nb, tt, d)

    @pl.when(t == pl.num_programs(1) - 1)
    def _store_state():
        sout_ref[...] = s_scr[...]


def _ret_mixer(x, cos, sin, decay, pos0, gmix, lw, states_in, states_out, layer, n_layers, *, nb, tt):
    bsz, seq, d = x.shape
    wq, wk, wv, wg, wo = lw["wq"], lw["wk"], lw["wv"], lw["wg"], lw["wo"]
    qkw, vw = wq.shape[1], wv.shape[1]
    dk, dv = qkw // RET_HEADS, vw // RET_HEADS
    chunk = decay[0].shape[1]
    assert pos0 % tt == 0 and seq % tt == 0 and tt % chunk == 0 and bsz % nb == 0
    has_state = states_in is not None
    has_prev = states_out is not None
    pos_blk = pos0 // tt
    in_specs = [
        pl.BlockSpec((nb, tt, d), lambda b, t: (b, t, 0)),
        pl.BlockSpec((tt, dk), lambda b, t: (pos_blk + t, 0)),
        pl.BlockSpec((tt, dk), lambda b, t: (pos_blk + t, 0)),
        _const_spec((RET_HEADS, chunk, chunk)),
        _const_spec((RET_HEADS, chunk, LANES)), _const_spec((RET_HEADS, chunk, LANES)),
        _const_spec((1, d)),
        _const_spec((d, qkw)), _const_spec((d, qkw)), _const_spec((d, vw)), _const_spec((d, vw)),
        _const_spec((1, vw)), _const_spec((1, vw)),
        _const_spec((vw, d)),
    ]
    args = [x, cos, sin, *decay, gmix, wq, wk, wv, wg, lw["gn_g"], lw["gn_b"], wo]
    state_spec = pl.BlockSpec((None, nb, RET_HEADS, dk, dv), lambda b, t: (layer, b, 0, 0, 0))
    aliases = {}
    if has_state:
        in_specs.append(state_spec)
        args.append(states_in)
    if has_prev:
        aliases[len(args)] = 1
        in_specs.append(pl.BlockSpec(memory_space=pl.ANY))
        args.append(states_out)
    return pl.pallas_call(
        functools.partial(_ret_kernel, nb=nb, tt=tt, chunk=chunk, has_state=has_state, has_prev=has_prev),
        grid=(bsz // nb, seq // tt),
        in_specs=in_specs,
        out_specs=[pl.BlockSpec((nb, tt, d), lambda b, t: (b, t, 0)), state_spec],
        out_shape=[jax.ShapeDtypeStruct(x.shape, x.dtype),
                   jax.ShapeDtypeStruct((n_layers, bsz, RET_HEADS, dk, dv), x.dtype)],
        scratch_shapes=[pltpu.VMEM((nb, RET_HEADS, dk, dv), F32),
                        pltpu.VMEM((nb * tt, vw), BF16)],
        input_output_aliases=aliases,
        compiler_params=pltpu.CompilerParams(dimension_semantics=("parallel", "arbitrary"),
                                             vmem_limit_bytes=VMEM_LIMIT_BYTES),
        name="ret_mixer_state" if has_state else "ret_mixer",
    )(*args)


def _conv_kernel(*refs, nb, tt, has_state, has_prev):
    refs = list(refs)
    x_ref, gmix_ref, win_ref, cw_ref, wo_ref = refs[:5]
    del refs[:5]
    h0_ref = refs.pop(0) if has_state else None
    if has_prev:
        refs.pop(0)
    o_ref, hout_ref, u_scr = refs
    d = x_ref.shape[-1]
    nh = CONV_WIDTH - 1
    head = SUBLANES
    t = pl.program_id(1)

    @pl.when(t == 0)
    def _load_hist():
        if has_state:
            u_scr[:, head - nh:head, :] = h0_ref[...]
        else:
            u_scr[:, head - nh:head, :] = jnp.zeros((nb, nh, d), F32)

    x = x_ref[...].reshape(nb * tt, d)
    xn = _rms(x, gmix_ref[...]).astype(BF16)
    bg = _dot(xn, win_ref[:, 0:d])
    u = _dot(xn, win_ref[:, d:2 * d]) * _dot(xn, win_ref[:, 2 * d:3 * d])
    u_scr[:, head:head + tt, :] = u.reshape(nb, tt, d)
    y = cw_ref[CONV_WIDTH - 1:CONV_WIDTH, :] * u
    for j in range(CONV_WIDTH - 1):
        off = head - nh + j
        y = y + cw_ref[j:j + 1, :] * u_scr[:, off:off + tt, :].reshape(nb * tt, d)
    z = _dot((bg * y).astype(BF16), wo_ref[...])
    o_ref[...] = (x + z).reshape(nb, tt, d)
    tail = u_scr[:, head + tt - nh:head + tt, :]
    u_scr[:, head - nh:head, :] = tail
    hout_ref[...] = tail


def _conv_mixer(x, gmix, lw, hist_in, hist_out, layer, n_layers, *, nb, tt):
    bsz, seq, d = x.shape
    nh = CONV_WIDTH - 1
    assert seq % tt == 0 and bsz % nb == 0 and tt >= nh
    has_state = hist_in is not None
    has_prev = hist_out is not None
    in_specs = [
        pl.BlockSpec((nb, tt, d), lambda b, t: (b, t, 0)),
        _const_spec((1, d)),
        _const_spec((d, 3 * d)),
        _const_spec((CONV_WIDTH, d)),
        _const_spec((d, d)),
    ]
    args = [x, gmix, lw["w_in"], lw["conv_w"], lw["wo"]]
    hist_spec = pl.BlockSpec((None, nb, nh, d), lambda b, t: (layer, b, 0, 0))
    aliases = {}
    if has_state:
        in_specs.append(hist_spec)
        args.append(hist_in)
    if has_prev:
        aliases[len(args)] = 1
        in_specs.append(pl.BlockSpec(memory_space=pl.ANY))
        args.append(hist_out)
    return pl.pallas_call(
        functools.partial(_conv_kernel, nb=nb, tt=tt, has_state=has_state, has_prev=has_prev),
        grid=(bsz // nb, seq // tt),
        in_specs=in_specs,
        out_specs=[pl.BlockSpec((nb, tt, d), lambda b, t: (b, t, 0)), hist_spec],
        out_shape=[jax.ShapeDtypeStruct(x.shape, x.dtype),
                   jax.ShapeDtypeStruct((n_layers, bsz, nh, d), x.dtype)],
        scratch_shapes=[pltpu.VMEM((nb, SUBLANES + tt, d), F32)],
        input_output_aliases=aliases,
        compiler_params=pltpu.CompilerParams(dimension_semantics=("parallel", "arbitrary"),
                                             vmem_limit_bytes=VMEM_LIMIT_BYTES),
        name="conv_mixer_state" if has_state else "conv_mixer",
    )(*args)


def _ffn_kernel(x_ref, p_ref, gmlp_ref, wup_ref, wdn_ref, gple_ref, wgate_ref, wproj_ref, gfin_ref,
                o_ref, u_scr, *, final):
    d = x_ref.shape[-1]
    dff = wup_ref.shape[1]
    x = x_ref[...]
    xn = _rms(x, gmlp_ref[...]).astype(BF16)
    for j in range(dff // d):
        u = jnp.maximum(_dot(xn, wup_ref[:, j * d:(j + 1) * d]), 0.0)
        u_scr[:, j * d:(j + 1) * d] = (u * u).astype(BF16)
    h1 = x + _dot(u_scr[...], wdn_ref[...])
    rows = x.shape[0] // FFN_TAIL_SPLIT
    for i in range(FFN_TAIL_SPLIT):
        rs = slice(i * rows, (i + 1) * rows)
        pp = _dot(p_ref[rs, :].astype(BF16), wproj_ref[...])
        gate = jax.nn.sigmoid(_dot(_rms(h1[rs], gple_ref[...]).astype(BF16), wgate_ref[...]))
        h2 = h1[rs] + pp * gate
        if final:
            h2 = _rms(h2, gfin_ref[...])
        o_ref[rs, :] = h2


def _ffn_ple(x, p, layer, fw, gfin, *, tile, final):
    n, d = x.shape
    dff = fw["w_up"].shape[1]
    pd = p.shape[-1]
    tile = min(tile, n)
    assert n % tile == 0
    return pl.pallas_call(
        functools.partial(_ffn_kernel, final=final),
        grid=(n // tile,),
        in_specs=[
            pl.BlockSpec((tile, d), lambda i: (i, 0)),
            pl.BlockSpec((None, tile, pd), lambda i: (layer, i, 0)),
            _const_spec((1, d)),
            _const_spec((d, dff)), _const_spec((dff, d)),
            _const_spec((1, d)),
            _const_spec((d, d)), _const_spec((pd, d)),
            _const_spec((1, d)),
        ],
        out_specs=pl.BlockSpec((tile, d), lambda i: (i, 0)),
        out_shape=jax.ShapeDtypeStruct(x.shape, x.dtype),
        scratch_shapes=[pltpu.VMEM((tile, dff), BF16)],
        compiler_params=pltpu.CompilerParams(dimension_semantics=("parallel",),
                                             vmem_limit_bytes=VMEM_LIMIT_BYTES),
        name="ffn_ple_final" if final else "ffn_ple",
    )(x, p, fw["g_mlp"], fw["w_up"], fw["w_down"], fw["g_ple"], fw["w_gate"], fw["w_proj"], gfin)


def _trunk(x, p, ret_states, conv_states, pos0, cos, sin, wts, *, nb, tt, conv_tt, ffn_tile):
    bsz, seq, d = x.shape
    depth = p.shape[0]
    n_ret, n_conv = len(wts["ret"]), len(wts["conv"])
    p = p.reshape(depth, bsz * seq, p.shape[-1])
    decay = _decay_tables(min(tt, RET_CHUNK))
    new_ret = new_conv = None
    h = x
    for i in range(depth):
        r = i // 2
        if i % 2 == 0:
            h, new_ret = _ret_mixer(h, cos, sin, decay, pos0, wts["g_mix"][i], wts["ret"][r], ret_states,
                                    new_ret, r, n_ret, nb=nb, tt=tt)
        else:
            h, new_conv = _conv_mixer(h, wts["g_mix"][i], wts["conv"][r], conv_states, new_conv,
                                      r, n_conv, nb=nb, tt=conv_tt)
        h = _ffn_ple(h.reshape(bsz * seq, d), p, i, wts["ffn"][i], wts["g_final"],
                     tile=ffn_tile, final=(i == depth - 1)).reshape(bsz, seq, d)
    return h, new_ret, new_conv


def kernel(x_prompt, x_sample, state_ret, state_conv, p_prompt, p_sample, g_mix, w_ret_in, ret_gn_g, ret_gn_b, w_ret_out, w_conv_in, conv_w, w_conv_out, g_mlp, w_up, w_down, g_ple, w_ple_gate, w_ple_proj, g_final):
    depth = p_prompt.shape[0]
    seq, dec_seq = x_prompt.shape[1], x_sample.shape[1]
    vw = w_ret_out.shape[1]
    qkw = (w_ret_in.shape[2] - 2 * vw) // 2
    dk = qkw // RET_HEADS

    row = lambda a: a.reshape(1, -1)
    wts = {
        "g_mix": [row(g_mix[i]) for i in range(depth)],
        "g_final": row(g_final),
        "ret": [], "conv": [], "ffn": [],
    }
    for r in range(w_ret_in.shape[0]):
        w = w_ret_in[r]
        wts["ret"].append({
            "wq": w[:, :qkw].astype(BF16),
            "wk": w[:, qkw:2 * qkw].astype(BF16),
            "wv": w[:, 2 * qkw:2 * qkw + vw].astype(BF16),
            "wg": w[:, 2 * qkw + vw:].astype(BF16),
            "gn_g": row(ret_gn_g[r]), "gn_b": row(ret_gn_b[r]),
            "wo": w_ret_out[r].astype(BF16),
        })
    for r in range(w_conv_in.shape[0]):
        wts["conv"].append({"w_in": w_conv_in[r].astype(BF16), "conv_w": conv_w[r],
                            "wo": w_conv_out[r].astype(BF16)})
    for i in range(depth):
        wts["ffn"].append({"g_mlp": row(g_mlp[i]), "w_up": w_up[i].astype(BF16),
                           "w_down": w_down[i].astype(BF16), "g_ple": row(g_ple[i]),
                           "w_gate": w_ple_gate[i].astype(BF16), "w_proj": w_ple_proj[i].astype(BF16)})

    n_pos = max(seq, PAST_LEN + dec_seq)
    n_pos = -(-n_pos // ROPE_TILE) * ROPE_TILE
    cos, sin = _rope_tables(n_pos, dk)

    y_p, ret_p, conv_p = _trunk(x_prompt, p_prompt, None, None, 0, cos, sin, wts,
                                nb=1, tt=SEQ_TILE, conv_tt=CONV_TILE, ffn_tile=FFN_TILE)
    y_s, ret_s, conv_s = _trunk(x_sample, p_sample, state_ret, state_conv, PAST_LEN, cos, sin, wts,
                                nb=2, tt=dec_seq, conv_tt=dec_seq, ffn_tile=FFN_TILE)
    return (y_p, y_s, ret_p, conv_p, ret_s, conv_s)
```

```python
import functools

import numpy as np
import jax
import jax.numpy as jnp
from jax import lax
from jax.experimental import pallas as pl
from jax.experimental.pallas import tpu as pltpu

F32 = jnp.float32
BF16 = jnp.bfloat16

RET_HEADS = 4
CONV_WIDTH = 3
ROPE_BASE = 10000.0
NORM_EPS = 1e-6
GN_EPS = 1e-6
PAST_LEN = 4096

LANES = 128
SUBLANES = 8
VMEM_LIMIT_BYTES = 56 * 1024 * 1024

SEQ_TILE = 512
CONV_TILE = 1024
FFN_TILE = 1024
FFN_TAIL_SPLIT = 4
RET_CHUNK = 256
ROPE_TILE = 512


def _const_spec(shape):
    nd = len(shape)
    return pl.BlockSpec(shape, lambda *_: (0,) * nd, pipeline_mode=pl.Buffered(1))


def _layer_spec(block, layer, col=0):
    return pl.BlockSpec((None,) + tuple(block), lambda *_: (layer, 0, col), pipeline_mode=pl.Buffered(1))


def _rms(x, g):
    ms = jnp.mean(x * x, axis=-1, keepdims=True)
    return (x * lax.rsqrt(ms + NORM_EPS)) * g


def _dot(a, b):
    return jnp.dot(a, b, preferred_element_type=F32)


def _rope_table_kernel(inv_ref, sign_ref, cos_ref, sin_ref):
    rows = cos_ref.shape[0]
    pos = lax.broadcasted_iota(jnp.int32, cos_ref.shape, 0) + pl.program_id(0) * rows
    ang = pos.astype(F32) * inv_ref[...]
    cos_ref[...] = jnp.cos(ang)
    sin_ref[...] = jnp.sin(ang) * sign_ref[...]


def _rope_tables(n_pos, dk):
    inv = 1.0 / (ROPE_BASE ** jnp.linspace(0.0, 1.0, dk // 2, dtype=F32))
    inv = jnp.repeat(inv, 2).reshape(1, dk)
    sign = jnp.tile(jnp.array([-1.0, 1.0], F32), dk // 2).reshape(1, dk)
    out = jax.ShapeDtypeStruct((n_pos, dk), F32)
    return pl.pallas_call(
        _rope_table_kernel,
        grid=(n_pos // ROPE_TILE,),
        in_specs=[pl.BlockSpec((1, dk), lambda i: (0, 0))] * 2,
        out_specs=[pl.BlockSpec((ROPE_TILE, dk), lambda i: (i, 0))] * 2,
        out_shape=[out, out],
        name="rope_tables",
    )(inv, sign)


def _log_gamma(h):
    return float(np.log(1.0 - 2.0 ** (-5.0 - h)))


def _decay_table_kernel(intra_ref, qdec_ref, kdec_ref):
    chunk = intra_ref.shape[1]
    diff = (lax.broadcasted_iota(jnp.int32, (chunk, chunk), 0)
            - lax.broadcasted_iota(jnp.int32, (chunk, chunk), 1))
    row = lax.broadcasted_iota(jnp.int32, (chunk, LANES), 0).astype(F32)
    for h in range(RET_HEADS):
        lg = _log_gamma(h)
        intra_ref[h] = jnp.where(diff >= 0, jnp.exp(jnp.maximum(diff, 0).astype(F32) * lg), 0.0)
        qdec_ref[h] = jnp.exp((row + 1.0) * lg)
        kdec_ref[h] = jnp.exp((chunk - 1.0 - row) * lg)


def _decay_tables(chunk):
    return pl.pallas_call(
        _decay_table_kernel,
        out_shape=[jax.ShapeDtypeStruct((RET_HEADS, chunk, chunk), F32),
                   jax.ShapeDtypeStruct((RET_HEADS, chunk, LANES), F32),
                   jax.ShapeDtypeStruct((RET_HEADS, chunk, LANES), F32)],
        name="decay_tables",
    )()


def _ret_kernel(*refs, nb, tt, chunk, has_state, has_prev, single_tile):
    refs = list(refs)
    (x_ref, cos_ref, sin_ref, intra_ref, qdec_ref, kdec_ref, gmix_ref, wq_ref, wk_ref, wv_ref, wg_ref,
     gng_ref, gnb_ref, wo_ref) = refs[:14]
    del refs[:14]
    s0_ref = refs.pop(0) if has_state else None
    if has_prev:
        refs.pop(0)
    o_ref, sout_ref, gated_scr = refs
    d = x_ref.shape[-1]
    dk = wq_ref.shape[1] // RET_HEADS
    dv = wv_ref.shape[1] // RET_HEADS
    t = pl.program_id(1)

    if not single_tile:
        @pl.when(t == 0)
        def _init_state():
            sout_ref[...] = s0_ref[...] if has_state else jnp.zeros_like(sout_ref)

    def state(b, h, first_chunk):
        if single_tile and first_chunk:
            return s0_ref[b, h] if has_state else jnp.zeros((dk, dv), F32)
        return sout_ref[b, h]

    x = x_ref[...].reshape(nb * tt, d)
    xn = _rms(x, gmix_ref[...]).astype(BF16)
    cos = cos_ref[...]
    sin = sin_ref[...]
    if nb > 1:
        cos = jnp.concatenate([cos] * nb, axis=0)
        sin = jnp.concatenate([sin] * nb, axis=0)
    even_lane = (lax.broadcasted_iota(jnp.int32, (nb * tt, LANES), 1) & 1) == 0
    groups = dk // LANES

    def rotate(z):
        parts = []
        for j in range(z.shape[1] // LANES):
            zj = z[:, j * LANES:(j + 1) * LANES]
            tj = j % groups
            partner = jnp.where(even_lane, pltpu.roll(zj, LANES - 1, 1), pltpu.roll(zj, 1, 1))
            parts.append(zj * cos[:, tj * LANES:(tj + 1) * LANES] + partner * sin[:, tj * LANES:(tj + 1) * LANES])
        return jnp.concatenate(parts, axis=1)

    def scale_rows(z, col):
        return jnp.concatenate([z[:, j * LANES:(j + 1) * LANES] * col for j in range(z.shape[1] // LANES)],
                               axis=1)

    q = rotate(_dot(xn, wq_ref[...])).astype(BF16)
    k = rotate(_dot(xn, wk_ref[...])) * (dk ** -0.5)
    v = _dot(xn, wv_ref[...]).astype(BF16)
    kb = k.astype(BF16)

    n_chunks = tt // chunk
    blocks = [(b, c) for b in range(nb) for c in range(n_chunks)]

    def rows(b, c):
        r0 = b * tt + c * chunk
        return slice(r0, r0 + chunk)

    scores = {}
    for (b, c) in blocks:
        for h in range(RET_HEADS):
            sc = lax.dot_general(q[rows(b, c), h * dk:(h + 1) * dk], kb[rows(b, c), h * dk:(h + 1) * dk],
                                 (((1,), (1,)), ((), ())), preferred_element_type=F32)
            scores[b, c, h] = (sc * intra_ref[h]).astype(BF16)

    outs = {}
    for (b, c) in blocks:
        for h in range(RET_HEADS):
            qc = q[rows(b, c), h * dk:(h + 1) * dk]
            vc = v[rows(b, c), h * dv:(h + 1) * dv]
            outs[b, c, h] = (_dot(scores[b, c, h], vc)
                             + scale_rows(_dot(qc, state(b, h, c == 0).astype(BF16)), qdec_ref[h]))
        for h in range(RET_HEADS):
            kd = scale_rows(k[rows(b, c), h * dk:(h + 1) * dk], kdec_ref[h]).astype(BF16)
            vc = v[rows(b, c), h * dv:(h + 1) * dv]
            sout_ref[b, h] = (state(b, h, c == 0) * float(np.exp(chunk * _log_gamma(h)))
                           + lax.dot_general(kd, vc, (((0,), (0,)), ((), ())), preferred_element_type=F32))

    for h in range(RET_HEADS):
        g = _dot(xn, wg_ref[:, h * dv:(h + 1) * dv])
        gn_g = gng_ref[:, h * dv:(h + 1) * dv]
        gn_b = gnb_ref[:, h * dv:(h + 1) * dv]
        for b in range(nb):
            chunks = [outs[b, c, h] for c in range(n_chunks)]
            o = chunks[0] if n_chunks == 1 else jnp.concatenate(chunks, axis=0)
            mu = jnp.mean(o, axis=-1, keepdims=True)
            oc = o - mu
            var = jnp.mean(oc * oc, axis=-1, keepdims=True)
            on = (oc * lax.rsqrt(var + GN_EPS)) * gn_g + gn_b
            gb = g[b * tt:(b + 1) * tt]
            gated = (gb * jax.nn.sigmoid(gb)) * on
            gated_scr[b * tt:(b + 1) * tt, h * dv:(h + 1) * dv] = gated.astype(BF16)

    y = _dot(gated_scr[...], wo_ref[...])
    o_ref[...] = (x + y).reshape(nb, tt, d)


def _ret_mixer(x, cos, sin, decay, pos0, wts, mix_layer, states_in, states_out, layer, *, nb, tt):
    bsz, seq, d = x.shape
    w_in, wo = wts["w_ret_in"], wts["w_ret_out"]
    n_layers, vw = wo.shape[0], wo.shape[1]
    qkw = (w_in.shape[2] - 2 * vw) // 2
    dk, dv = qkw // RET_HEADS, vw // RET_HEADS
    chunk = decay[0].shape[1]
    assert pos0 % tt == 0 and seq % tt == 0 and tt % chunk == 0 and bsz % nb == 0
    assert (2 * qkw) % vw == 0
    v_col = 2 * qkw // vw
    has_state = states_in is not None
    has_prev = states_out is not None
    pos_blk = pos0 // tt
    in_specs = [
        pl.BlockSpec((nb, tt, d), lambda b, t: (b, t, 0)),
        pl.BlockSpec((tt, dk), lambda b, t: (pos_blk + t, 0)),
        pl.BlockSpec((tt, dk), lambda b, t: (pos_blk + t, 0)),
        _const_spec((RET_HEADS, chunk, chunk)),
        _const_spec((RET_HEADS, chunk, LANES)), _const_spec((RET_HEADS, chunk, LANES)),
        _layer_spec((1, d), mix_layer),
        _layer_spec((d, qkw), layer, 0), _layer_spec((d, qkw), layer, 1),
        _layer_spec((d, vw), layer, v_col), _layer_spec((d, vw), layer, v_col + 1),
        _layer_spec((1, vw), layer), _layer_spec((1, vw), layer),
        _layer_spec((vw, d), layer),
    ]
    args = [x, cos, sin, *decay, wts["g_mix"], w_in, w_in, w_in, w_in, wts["ret_gn_g"], wts["ret_gn_b"], wo]
    state_spec = pl.BlockSpec((None, nb, RET_HEADS, dk, dv), lambda b, t: (layer, b, 0, 0, 0))
    aliases = {}
    if has_state:
        in_specs.append(state_spec)
        args.append(states_in)
    if has_prev:
        aliases[len(args)] = 1
        in_specs.append(pl.BlockSpec(memory_space=pl.ANY))
        args.append(states_out)
    return pl.pallas_call(
        functools.partial(_ret_kernel, nb=nb, tt=tt, chunk=chunk, has_state=has_state, has_prev=has_prev,
                          single_tile=(seq == tt)),
        grid=(bsz // nb, seq // tt),
        in_specs=in_specs,
        out_specs=[pl.BlockSpec((nb, tt, d), lambda b, t: (b, t, 0)), state_spec],
        out_shape=[jax.ShapeDtypeStruct(x.shape, x.dtype),
                   jax.ShapeDtypeStruct((n_layers, bsz, RET_HEADS, dk, dv), x.dtype)],
        scratch_shapes=[pltpu.VMEM((nb * tt, vw), BF16)],
        input_output_aliases=aliases,
        compiler_params=pltpu.CompilerParams(dimension_semantics=("parallel", "arbitrary"),
                                             vmem_limit_bytes=VMEM_LIMIT_BYTES),
        name="ret_mixer_state" if has_state else "ret_mixer",
    )(*args)


def _conv_kernel(*refs, nb, tt, has_state, has_prev):
    refs = list(refs)
    x_ref, gmix_ref, win_ref, cw_ref, wo_ref = refs[:5]
    del refs[:5]
    h0_ref = refs.pop(0) if has_state else None
    if has_prev:
        refs.pop(0)
    o_ref, hout_ref, u_scr = refs
    d = x_ref.shape[-1]
    nh = CONV_WIDTH - 1
    head = SUBLANES
    t = pl.program_id(1)

    @pl.when(t == 0)
    def _load_hist():
        if has_state:
            u_scr[:, head - nh:head, :] = h0_ref[...]
        else:
            u_scr[:, head - nh:head, :] = jnp.zeros((nb, nh, d), F32)

    x = x_ref[...].reshape(nb * tt, d)
    xn = _rms(x, gmix_ref[...]).astype(BF16)
    bg = _dot(xn, win_ref[:, 0:d])
    u = _dot(xn, win_ref[:, d:2 * d]) * _dot(xn, win_ref[:, 2 * d:3 * d])
    u_scr[:, head:head + tt, :] = u.reshape(nb, tt, d)
    y = cw_ref[CONV_WIDTH - 1:CONV_WIDTH, :] * u
    for j in range(CONV_WIDTH - 1):
        off = head - nh + j
        y = y + cw_ref[j:j + 1, :] * u_scr[:, off:off + tt, :].reshape(nb * tt, d)
    z = _dot((bg * y).astype(BF16), wo_ref[...])
    o_ref[...] = (x + z).reshape(nb, tt, d)
    tail = u_scr[:, head + tt - nh:head + tt, :]
    u_scr[:, head - nh:head, :] = tail
    hout_ref[...] = tail


def _conv_mixer(x, wts, mix_layer, hist_in, hist_out, layer, *, nb, tt):
    bsz, seq, d = x.shape
    nh = CONV_WIDTH - 1
    n_layers = wts["w_conv_in"].shape[0]
    assert seq % tt == 0 and bsz % nb == 0 and tt >= nh
    has_state = hist_in is not None
    has_prev = hist_out is not None
    in_specs = [
        pl.BlockSpec((nb, tt, d), lambda b, t: (b, t, 0)),
        _layer_spec((1, d), mix_layer),
        _layer_spec((d, 3 * d), layer),
        _layer_spec((CONV_WIDTH, d), layer),
        _layer_spec((d, d), layer),
    ]
    args = [x, wts["g_mix"], wts["w_conv_in"], wts["conv_w"], wts["w_conv_out"]]
    hist_spec = pl.BlockSpec((None, nb, nh, d), lambda b, t: (layer, b, 0, 0))
    aliases = {}
    if has_state:
        in_specs.append(hist_spec)
        args.append(hist_in)
    if has_prev:
        aliases[len(args)] = 1
        in_specs.append(pl.BlockSpec(memory_space=pl.ANY))
        args.append(hist_out)
    return pl.pallas_call(
        functools.partial(_conv_kernel, nb=nb, tt=tt, has_state=has_state, has_prev=has_prev),
        grid=(bsz // nb, seq // tt),
        in_specs=in_specs,
        out_specs=[pl.BlockSpec((nb, tt, d), lambda b, t: (b, t, 0)), hist_spec],
        out_shape=[jax.ShapeDtypeStruct(x.shape, x.dtype),
                   jax.ShapeDtypeStruct((n_layers, bsz, nh, d), x.dtype)],
        scratch_shapes=[pltpu.VMEM((nb, SUBLANES + tt, d), F32)],
        input_output_aliases=aliases,
        compiler_params=pltpu.CompilerParams(dimension_semantics=("parallel", "arbitrary"),
                                             vmem_limit_bytes=VMEM_LIMIT_BYTES),
        name="conv_mixer_state" if has_state else "conv_mixer",
    )(*args)


def _ffn_kernel(x_ref, p_ref, gmlp_ref, wup_ref, wdn_ref, gple_ref, wgate_ref, wproj_ref, gfin_ref,
                o_ref, u_scr, *, final):
    d = x_ref.shape[-1]
    dff = wup_ref.shape[1]
    x = x_ref[...]
    xn = _rms(x, gmlp_ref[...]).astype(BF16)
    for j in range(dff // d):
        u = jnp.maximum(_dot(xn, wup_ref[:, j * d:(j + 1) * d]), 0.0)
        u_scr[:, j * d:(j + 1) * d] = (u * u).astype(BF16)
    h1 = x + _dot(u_scr[...], wdn_ref[...])
    rows = x.shape[0] // FFN_TAIL_SPLIT
    for i in range(FFN_TAIL_SPLIT):
        rs = slice(i * rows, (i + 1) * rows)
        pp = _dot(p_ref[rs, :].astype(BF16), wproj_ref[...])
        gate = jax.nn.sigmoid(_dot(_rms(h1[rs], gple_ref[...]).astype(BF16), wgate_ref[...]))
        h2 = h1[rs] + pp * gate
        if final:
            h2 = _rms(h2, gfin_ref[...])
        o_ref[rs, :] = h2


def _ffn_ple(x, p, layer, wts, *, tile, final):
    n, d = x.shape
    dff = wts["w_up"].shape[2]
    pd = p.shape[-1]
    tile = min(tile, n)
    assert n % tile == 0
    return pl.pallas_call(
        functools.partial(_ffn_kernel, final=final),
        grid=(n // tile,),
        in_specs=[
            pl.BlockSpec((tile, d), lambda i: (i, 0)),
            pl.BlockSpec((None, tile, pd), lambda i: (layer, i, 0)),
            _layer_spec((1, d), layer),
            _layer_spec((d, dff), layer), _layer_spec((dff, d), layer),
            _layer_spec((1, d), layer),
            _layer_spec((d, d), layer), _layer_spec((pd, d), layer),
            _const_spec((1, d)),
        ],
        out_specs=pl.BlockSpec((tile, d), lambda i: (i, 0)),
        out_shape=jax.ShapeDtypeStruct(x.shape, x.dtype),
        scratch_shapes=[pltpu.VMEM((tile, dff), BF16)],
        compiler_params=pltpu.CompilerParams(dimension_semantics=("parallel",),
                                             vmem_limit_bytes=VMEM_LIMIT_BYTES),
        name="ffn_ple_final" if final else "ffn_ple",
    )(x, p, wts["g_mlp"], wts["w_up"], wts["w_down"], wts["g_ple"], wts["w_ple_gate"], wts["w_ple_proj"],
      wts["g_final"])


def _trunk(x, p, ret_states, conv_states, pos0, cos, sin, wts, *, nb, tt, conv_tt, ffn_tile):
    bsz, seq, d = x.shape
    depth = p.shape[0]
    p = p.reshape(depth, bsz * seq, p.shape[-1])
    decay = _decay_tables(min(tt, RET_CHUNK))
    new_ret = new_conv = None
    h = x
    for i in range(depth):
        r = i // 2
        if i % 2 == 0:
            h, new_ret = _ret_mixer(h, cos, sin, decay, pos0, wts, i, ret_states, new_ret, r, nb=nb, tt=tt)
        else:
            h, new_conv = _conv_mixer(h, wts, i, conv_states, new_conv, r, nb=nb, tt=conv_tt)
        h = _ffn_ple(h.reshape(bsz * seq, d), p, i, wts, tile=ffn_tile,
                     final=(i == depth - 1)).reshape(bsz, seq, d)
    return h, new_ret, new_conv


def kernel(x_prompt, x_sample, state_ret, state_conv, p_prompt, p_sample, g_mix, w_ret_in, ret_gn_g, ret_gn_b, w_ret_out, w_conv_in, conv_w, w_conv_out, g_mlp, w_up, w_down, g_ple, w_ple_gate, w_ple_proj, g_final):
    depth = p_prompt.shape[0]
    seq, dec_seq = x_prompt.shape[1], x_sample.shape[1]
    vw = w_ret_out.shape[1]
    qkw = (w_ret_in.shape[2] - 2 * vw) // 2
    dk = qkw // RET_HEADS

    rows = lambda a: a.reshape(a.shape[0], 1, a.shape[1])
    wts = {
        "g_mix": rows(g_mix), "g_mlp": rows(g_mlp), "g_ple": rows(g_ple), "g_final": g_final.reshape(1, -1),
        "ret_gn_g": rows(ret_gn_g), "ret_gn_b": rows(ret_gn_b), "conv_w": conv_w,
        "w_ret_in": w_ret_in.astype(BF16), "w_ret_out": w_ret_out.astype(BF16),
        "w_conv_in": w_conv_in.astype(BF16), "w_conv_out": w_conv_out.astype(BF16),
        "w_up": w_up.astype(BF16), "w_down": w_down.astype(BF16),
        "w_ple_gate": w_ple_gate.astype(BF16), "w_ple_proj": w_ple_proj.astype(BF16),
    }

    n_pos = max(seq, PAST_LEN + dec_seq)
    n_pos = -(-n_pos // ROPE_TILE) * ROPE_TILE
    cos, sin = _rope_tables(n_pos, dk)

    y_p, ret_p, conv_p = _trunk(x_prompt, p_prompt, None, None, 0, cos, sin, wts,
                                nb=1, tt=SEQ_TILE, conv_tt=CONV_TILE, ffn_tile=FFN_TILE)
    y_s, ret_s, conv_s = _trunk(x_sample, p_sample, state_ret, state_conv, PAST_LEN, cos, sin, wts,
                                nb=2, tt=dec_seq, conv_tt=dec_seq, ffn_tile=FFN_TILE)
    return (y_p, y_s, ret_p, conv_p, ret_s, conv_s)
```

```python
import functools

import numpy as np
import jax
import jax.numpy as jnp
from jax import lax
from jax.experimental import pallas as pl
from jax.experimental.pallas import tpu as pltpu

F32 = jnp.float32
BF16 = jnp.bfloat16

RET_HEADS = 4
CONV_WIDTH = 3
ROPE_BASE = 10000.0
NORM_EPS = 1e-6
GN_EPS = 1e-6
PAST_LEN = 4096

LANES = 128
SUBLANES = 8
VMEM_LIMIT_BYTES = 56 * 1024 * 1024

SEQ_TILE = 1024
RET_SUB_TILE = 512
CONV_TILE = 1024
FFN_TILE = 1024
FFN_TAIL_SPLIT = 4
RET_CHUNK = 256
ROPE_TILE = 512


def _const_spec(shape):
    nd = len(shape)
    return pl.BlockSpec(shape, lambda *_: (0,) * nd, pipeline_mode=pl.Buffered(1))


def _layer_spec(block, layer, col=0):
    return pl.BlockSpec((None,) + tuple(block), lambda *_: (layer, 0, col), pipeline_mode=pl.Buffered(1))


def _rms(x, g):
    ms = jnp.mean(x * x, axis=-1, keepdims=True)
    return (x * lax.rsqrt(ms + NORM_EPS)) * g


def _dot(a, b):
    return jnp.dot(a, b, preferred_element_type=F32)


def _rope_table_kernel(inv_ref, sign_ref, cos_ref, sin_ref):
    rows = cos_ref.shape[0]
    pos = lax.broadcasted_iota(jnp.int32, cos_ref.shape, 0) + pl.program_id(0) * rows
    ang = pos.astype(F32) * inv_ref[...]
    cos_ref[...] = jnp.cos(ang)
    sin_ref[...] = jnp.sin(ang) * sign_ref[...]


def _rope_tables(n_pos, dk):
    inv = 1.0 / (ROPE_BASE ** jnp.linspace(0.0, 1.0, dk // 2, dtype=F32))
    inv = jnp.repeat(inv, 2).reshape(1, dk)
    sign = jnp.tile(jnp.array([-1.0, 1.0], F32), dk // 2).reshape(1, dk)
    out = jax.ShapeDtypeStruct((n_pos, dk), F32)
    return pl.pallas_call(
        _rope_table_kernel,
        grid=(n_pos // ROPE_TILE,),
        in_specs=[pl.BlockSpec((1, dk), lambda i: (0, 0))] * 2,
        out_specs=[pl.BlockSpec((ROPE_TILE, dk), lambda i: (i, 0))] * 2,
        out_shape=[out, out],
        name="rope_tables",
    )(inv, sign)


def _log_gamma(h):
    return float(np.log(1.0 - 2.0 ** (-5.0 - h)))


def _decay_table_kernel(intra_ref, qdec_ref, kdec_ref):
    chunk = intra_ref.shape[1]
    diff = (lax.broadcasted_iota(jnp.int32, (chunk, chunk), 0)
            - lax.broadcasted_iota(jnp.int32, (chunk, chunk), 1))
    row = lax.broadcasted_iota(jnp.int32, (chunk, LANES), 0).astype(F32)
    for h in range(RET_HEADS):
        lg = _log_gamma(h)
        intra_ref[h] = jnp.where(diff >= 0, jnp.exp(jnp.maximum(diff, 0).astype(F32) * lg), 0.0)
        qdec_ref[h] = jnp.exp((row + 1.0) * lg)
        kdec_ref[h] = jnp.exp((chunk - 1.0 - row) * lg)


def _decay_tables(chunk):
    return pl.pallas_call(
        _decay_table_kernel,
        out_shape=[jax.ShapeDtypeStruct((RET_HEADS, chunk, chunk), F32),
                   jax.ShapeDtypeStruct((RET_HEADS, chunk, LANES), F32),
                   jax.ShapeDtypeStruct((RET_HEADS, chunk, LANES), F32)],
        name="decay_tables",
    )()


def _ret_kernel(*refs, nb, tt, st, chunk, has_state, has_prev, single_tile):
    refs = list(refs)
    (x_ref, cos_ref, sin_ref, intra_ref, qdec_ref, kdec_ref, gmix_ref, wq_ref, wk_ref, wv_ref, wg_ref,
     gng_ref, gnb_ref, wo_ref) = refs[:14]
    del refs[:14]
    s0_ref = refs.pop(0) if has_state else None
    if has_prev:
        refs.pop(0)
    o_ref, sout_ref, gated_scr = refs
    d = x_ref.shape[-1]
    dk = wq_ref.shape[1] // RET_HEADS
    dv = wv_ref.shape[1] // RET_HEADS
    t = pl.program_id(1)

    if not single_tile:
        @pl.when(t == 0)
        def _init_state():
            sout_ref[...] = s0_ref[...] if has_state else jnp.zeros_like(sout_ref)

    def state(b, h, first_chunk):
        if single_tile and first_chunk:
            return s0_ref[b, h] if has_state else jnp.zeros((dk, dv), F32)
        return sout_ref[b, h]

    even_lane = (lax.broadcasted_iota(jnp.int32, (nb * st, LANES), 1) & 1) == 0
    groups = dk // LANES
    n_chunks = st // chunk
    blocks = [(b, c) for b in range(nb) for c in range(n_chunks)]

    def rows(b, c):
        r0 = b * st + c * chunk
        return slice(r0, r0 + chunk)

    def scale_rows(z, col):
        return jnp.concatenate([z[:, j * LANES:(j + 1) * LANES] * col for j in range(z.shape[1] // LANES)],
                               axis=1)

    for s in range(tt // st):
        x = x_ref[:, s * st:(s + 1) * st, :].reshape(nb * st, d)
        xn = _rms(x, gmix_ref[...]).astype(BF16)
        cos = cos_ref[s * st:(s + 1) * st, :]
        sin = sin_ref[s * st:(s + 1) * st, :]
        if nb > 1:
            cos = jnp.concatenate([cos] * nb, axis=0)
            sin = jnp.concatenate([sin] * nb, axis=0)

        def rotate(z):
            parts = []
            for j in range(z.shape[1] // LANES):
                zj = z[:, j * LANES:(j + 1) * LANES]
                tj = j % groups
                partner = jnp.where(even_lane, pltpu.roll(zj, LANES - 1, 1), pltpu.roll(zj, 1, 1))
                parts.append(zj * cos[:, tj * LANES:(tj + 1) * LANES]
                             + partner * sin[:, tj * LANES:(tj + 1) * LANES])
            return jnp.concatenate(parts, axis=1)

        q = rotate(_dot(xn, wq_ref[...])).astype(BF16)
        k = rotate(_dot(xn, wk_ref[...])) * (dk ** -0.5)
        v = _dot(xn, wv_ref[...]).astype(BF16)
        kb = k.astype(BF16)

        scores = {}
        for (b, c) in blocks:
            for h in range(RET_HEADS):
                sc = lax.dot_general(q[rows(b, c), h * dk:(h + 1) * dk], kb[rows(b, c), h * dk:(h + 1) * dk],
                                     (((1,), (1,)), ((), ())), preferred_element_type=F32)
                scores[b, c, h] = (sc * intra_ref[h]).astype(BF16)

        outs = {}
        for (b, c) in blocks:
            first = s == 0 and c == 0
            for h in range(RET_HEADS):
                qc = q[rows(b, c), h * dk:(h + 1) * dk]
                vc = v[rows(b, c), h * dv:(h + 1) * dv]
                outs[b, c, h] = (_dot(scores[b, c, h], vc)
                                 + scale_rows(_dot(qc, state(b, h, first).astype(BF16)), qdec_ref[h]))
            for h in range(RET_HEADS):
                kd = scale_rows(k[rows(b, c), h * dk:(h + 1) * dk], kdec_ref[h]).astype(BF16)
                vc = v[rows(b, c), h * dv:(h + 1) * dv]
                sout_ref[b, h] = (state(b, h, first) * float(np.exp(chunk * _log_gamma(h)))
                                  + lax.dot_general(kd, vc, (((0,), (0,)), ((), ())),
                                                    preferred_element_type=F32))

        for h in range(RET_HEADS):
            g = _dot(xn, wg_ref[:, h * dv:(h + 1) * dv])
            gn_g = gng_ref[:, h * dv:(h + 1) * dv]
            gn_b = gnb_ref[:, h * dv:(h + 1) * dv]
            for b in range(nb):
                chunks = [outs[b, c, h] for c in range(n_chunks)]
                o = chunks[0] if n_chunks == 1 else jnp.concatenate(chunks, axis=0)
                mu = jnp.mean(o, axis=-1, keepdims=True)
                oc = o - mu
                var = jnp.mean(oc * oc, axis=-1, keepdims=True)
                on = (oc * lax.rsqrt(var + GN_EPS)) * gn_g + gn_b
                gb = g[b * st:(b + 1) * st]
                gated = (gb * jax.nn.sigmoid(gb)) * on
                gated_scr[b * st:(b + 1) * st, h * dv:(h + 1) * dv] = gated.astype(BF16)

        y = _dot(gated_scr[...], wo_ref[...])
        o_ref[:, s * st:(s + 1) * st, :] = (x + y).reshape(nb, st, d)


def _ret_mixer(x, cos, sin, decay, pos0, wts, mix_layer, states_in, states_out, layer, *, nb, tt):
    bsz, seq, d = x.shape
    w_in, wo = wts["w_ret_in"], wts["w_ret_out"]
    n_layers, vw = wo.shape[0], wo.shape[1]
    qkw = (w_in.shape[2] - 2 * vw) // 2
    dk, dv = qkw // RET_HEADS, vw // RET_HEADS
    chunk = decay[0].shape[1]
    st = min(tt, RET_SUB_TILE)
    assert pos0 % tt == 0 and seq % tt == 0 and tt % st == 0 and st % chunk == 0 and bsz % nb == 0
    assert (2 * qkw) % vw == 0
    v_col = 2 * qkw // vw
    has_state = states_in is not None
    has_prev = states_out is not None
    pos_blk = pos0 // tt
    in_specs = [
        pl.BlockSpec((nb, tt, d), lambda b, t: (b, t, 0)),
        pl.BlockSpec((tt, dk), lambda b, t: (pos_blk + t, 0)),
        pl.BlockSpec((tt, dk), lambda b, t: (pos_blk + t, 0)),
        _const_spec((RET_HEADS, chunk, chunk)),
        _const_spec((RET_HEADS, chunk, LANES)), _const_spec((RET_HEADS, chunk, LANES)),
        _layer_spec((1, d), mix_layer),
        _layer_spec((d, qkw), layer, 0), _layer_spec((d, qkw), layer, 1),
        _layer_spec((d, vw), layer, v_col), _layer_spec((d, vw), layer, v_col + 1),
        _layer_spec((1, vw), layer), _layer_spec((1, vw), layer),
        _layer_spec((vw, d), layer),
    ]
    args = [x, cos, sin, *decay, wts["g_mix"], w_in, w_in, w_in, w_in, wts["ret_gn_g"], wts["ret_gn_b"], wo]
    state_spec = pl.BlockSpec((None, nb, RET_HEADS, dk, dv), lambda b, t: (layer, b, 0, 0, 0))
    aliases = {}
    if has_state:
        in_specs.append(state_spec)
        args.append(states_in)
    if has_prev:
        aliases[len(args)] = 1
        in_specs.append(pl.BlockSpec(memory_space=pl.ANY))
        args.append(states_out)
    return pl.pallas_call(
        functools.partial(_ret_kernel, nb=nb, tt=tt, st=st, chunk=chunk, has_state=has_state,
                          has_prev=has_prev, single_tile=(seq == tt)),
        grid=(bsz // nb, seq // tt),
        in_specs=in_specs,
        out_specs=[pl.BlockSpec((nb, tt, d), lambda b, t: (b, t, 0)), state_spec],
        out_shape=[jax.ShapeDtypeStruct(x.shape, x.dtype),
                   jax.ShapeDtypeStruct((n_layers, bsz, RET_HEADS, dk, dv), x.dtype)],
        scratch_shapes=[pltpu.VMEM((nb * st, vw), BF16)],
        input_output_aliases=aliases,
        compiler_params=pltpu.CompilerParams(dimension_semantics=("parallel", "arbitrary"),
                                             vmem_limit_bytes=VMEM_LIMIT_BYTES),
        name="ret_mixer_state" if has_state else "ret_mixer",
    )(*args)


def _conv_kernel(*refs, nb, tt, has_state, has_prev):
    refs = list(refs)
    x_ref, gmix_ref, win_ref, cw_ref, wo_ref = refs[:5]
    del refs[:5]
    h0_ref = refs.pop(0) if has_state else None
    if has_prev:
        refs.pop(0)
    o_ref, hout_ref, u_scr = refs
    d = x_ref.shape[-1]
    nh = CONV_WIDTH - 1
    head = SUBLANES
    t = pl.program_id(1)

    @pl.when(t == 0)
    def _load_hist():
        if has_state:
            u_scr[:, head - nh:head, :] = h0_ref[...]
        else:
            u_scr[:, head - nh:head, :] = jnp.zeros((nb, nh, d), F32)

    x = x_ref[...].reshape(nb * tt, d)
    xn = _rms(x, gmix_ref[...]).astype(BF16)
    u = _dot(xn, win_ref[:, d:2 * d]) * _dot(xn, win_ref[:, 2 * d:3 * d])
    u_scr[:, head:head + tt, :] = u.reshape(nb, tt, d)
    bg = _dot(xn, win_ref[:, 0:d])
    y = cw_ref[CONV_WIDTH - 1:CONV_WIDTH, :] * u
    for j in range(CONV_WIDTH - 1):
        off = head - nh + j
        y = y + cw_ref[j:j + 1, :] * u_scr[:, off:off + tt, :].reshape(nb * tt, d)
    z = _dot((bg * y).astype(BF16), wo_ref[...])
    o_ref[...] = (x + z).reshape(nb, tt, d)
    tail = u_scr[:, head + tt - nh:head + tt, :]
    u_scr[:, head - nh:head, :] = tail
    hout_ref[...] = tail


def _conv_mixer(x, wts, mix_layer, hist_in, hist_out, layer, *, nb, tt):
    bsz, seq, d = x.shape
    nh = CONV_WIDTH - 1
    n_layers = wts["w_conv_in"].shape[0]
    assert seq % tt == 0 and bsz % nb == 0 and tt >= nh
    has_state = hist_in is not None
    has_prev = hist_out is not None
    in_specs = [
        pl.BlockSpec((nb, tt, d), lambda b, t: (b, t, 0)),
        _layer_spec((1, d), mix_layer),
        _layer_spec((d, 3 * d), layer),
        _layer_spec((CONV_WIDTH, d), layer),
        _layer_spec((d, d), layer),
    ]
    args = [x, wts["g_mix"], wts["w_conv_in"], wts["conv_w"], wts["w_conv_out"]]
    hist_spec = pl.BlockSpec((None, nb, nh, d), lambda b, t: (layer, b, 0, 0))
    aliases = {}
    if has_state:
        in_specs.append(hist_spec)
        args.append(hist_in)
    if has_prev:
        aliases[len(args)] = 1
        in_specs.append(pl.BlockSpec(memory_space=pl.ANY))
        args.append(hist_out)
    return pl.pallas_call(
        functools.partial(_conv_kernel, nb=nb, tt=tt, has_state=has_state, has_prev=has_prev),
        grid=(bsz // nb, seq // tt),
        in_specs=in_specs,
        out_specs=[pl.BlockSpec((nb, tt, d), lambda b, t: (b, t, 0)), hist_spec],
        out_shape=[jax.ShapeDtypeStruct(x.shape, x.dtype),
                   jax.ShapeDtypeStruct((n_layers, bsz, nh, d), x.dtype)],
        scratch_shapes=[pltpu.VMEM((nb, SUBLANES + tt, d), F32)],
        input_output_aliases=aliases,
        compiler_params=pltpu.CompilerParams(dimension_semantics=("parallel", "arbitrary"),
                                             vmem_limit_bytes=VMEM_LIMIT_BYTES),
        name="conv_mixer_state" if has_state else "conv_mixer",
    )(*args)


def _ffn_kernel(x_ref, p_ref, gmlp_ref, wup_ref, wdn_ref, gple_ref, wgate_ref, wproj_ref, gfin_ref,
                o_ref, u_scr, *, final):
    d = x_ref.shape[-1]
    dff = wup_ref.shape[1]
    x = x_ref[...]
    xn = _rms(x, gmlp_ref[...]).astype(BF16)
    for j in range(dff // d):
        u = jnp.maximum(_dot(xn, wup_ref[:, j * d:(j + 1) * d]), 0.0)
        u_scr[:, j * d:(j + 1) * d] = (u * u).astype(BF16)
    h1 = x + _dot(u_scr[...], wdn_ref[...])
    rows = x.shape[0] // FFN_TAIL_SPLIT
    for i in range(FFN_TAIL_SPLIT):
        rs = slice(i * rows, (i + 1) * rows)
        pp = _dot(p_ref[rs, :].astype(BF16), wproj_ref[...])
        gate = jax.nn.sigmoid(_dot(_rms(h1[rs], gple_ref[...]).astype(BF16), wgate_ref[...]))
        h2 = h1[rs] + pp * gate
        if final:
            h2 = _rms(h2, gfin_ref[...])
        o_ref[rs, :] = h2


def _ffn_ple(x, p, layer, wts, *, tile, final):
    n, d = x.shape
    dff = wts["w_up"].shape[2]
    pd = p.shape[-1]
    tile = min(tile, n)
    assert n % tile == 0
    return pl.pallas_call(
        functools.partial(_ffn_kernel, final=final),
        grid=(n // tile,),
        in_specs=[
            pl.BlockSpec((tile, d), lambda i: (i, 0)),
            pl.BlockSpec((None, tile, pd), lambda i: (layer, i, 0)),
            _layer_spec((1, d), layer),
            _layer_spec((d, dff), layer), _layer_spec((dff, d), layer),
            _layer_spec((1, d), layer),
            _layer_spec((d, d), layer), _layer_spec((pd, d), layer),
            _const_spec((1, d)),
        ],
        out_specs=pl.BlockSpec((tile, d), lambda i: (i, 0)),
        out_shape=jax.ShapeDtypeStruct(x.shape, x.dtype),
        scratch_shapes=[pltpu.VMEM((tile, dff), BF16)],
        compiler_params=pltpu.CompilerParams(dimension_semantics=("parallel",),
                                             vmem_limit_bytes=VMEM_LIMIT_BYTES),
        name="ffn_ple_final" if final else "ffn_ple",
    )(x, p, wts["g_mlp"], wts["w_up"], wts["w_down"], wts["g_ple"], wts["w_ple_gate"], wts["w_ple_proj"],
      wts["g_final"])


def _trunk(x, p, ret_states, conv_states, pos0, cos, sin, wts, *, nb, tt, conv_tt, ffn_tile):
    bsz, seq, d = x.shape
    depth = p.shape[0]
    p = p.reshape(depth, bsz * seq, p.shape[-1])
    decay = _decay_tables(min(tt, RET_CHUNK))
    new_ret = new_conv = None
    h = x
    for i in range(depth):
        r = i // 2
        if i % 2 == 0:
            h, new_ret = _ret_mixer(h, cos, sin, decay, pos0, wts, i, ret_states, new_ret, r, nb=nb, tt=tt)
        else:
            h, new_conv = _conv_mixer(h, wts, i, conv_states, new_conv, r, nb=nb, tt=conv_tt)
        h = _ffn_ple(h.reshape(bsz * seq, d), p, i, wts, tile=ffn_tile,
                     final=(i == depth - 1)).reshape(bsz, seq, d)
    return h, new_ret, new_conv


def kernel(x_prompt, x_sample, state_ret, state_conv, p_prompt, p_sample, g_mix, w_ret_in, ret_gn_g, ret_gn_b, w_ret_out, w_conv_in, conv_w, w_conv_out, g_mlp, w_up, w_down, g_ple, w_ple_gate, w_ple_proj, g_final):
    depth = p_prompt.shape[0]
    seq, dec_seq = x_prompt.shape[1], x_sample.shape[1]
    vw = w_ret_out.shape[1]
    qkw = (w_ret_in.shape[2] - 2 * vw) // 2
    dk = qkw // RET_HEADS

    rows = lambda a: a.reshape(a.shape[0], 1, a.shape[1])
    wts = {
        "g_mix": rows(g_mix), "g_mlp": rows(g_mlp), "g_ple": rows(g_ple), "g_final": g_final.reshape(1, -1),
        "ret_gn_g": rows(ret_gn_g), "ret_gn_b": rows(ret_gn_b), "conv_w": conv_w,
        "w_ret_in": w_ret_in.astype(BF16), "w_ret_out": w_ret_out.astype(BF16),
        "w_conv_in": w_conv_in.astype(BF16), "w_conv_out": w_conv_out.astype(BF16),
        "w_up": w_up.astype(BF16), "w_down": w_down.astype(BF16),
        "w_ple_gate": w_ple_gate.astype(BF16), "w_ple_proj": w_ple_proj.astype(BF16),
    }

    n_pos = max(seq, PAST_LEN + dec_seq)
    n_pos = -(-n_pos // ROPE_TILE) * ROPE_TILE
    cos, sin = _rope_tables(n_pos, dk)

    y_p, ret_p, conv_p = _trunk(x_prompt, p_prompt, None, None, 0, cos, sin, wts,
                                nb=1, tt=SEQ_TILE, conv_tt=CONV_TILE, ffn_tile=FFN_TILE)
    y_s, ret_s, conv_s = _trunk(x_sample, p_sample, state_ret, state_conv, PAST_LEN, cos, sin, wts,
                                nb=2, tt=dec_seq, conv_tt=dec_seq, ffn_tile=FFN_TILE)
    return (y_p, y_s, ret_p, conv_p, ret_s, conv_s)
```

```python
import functools

import numpy as np
import jax
import jax.numpy as jnp
from jax import lax
from jax.experimental import pallas as pl
from jax.experimental.pallas import tpu as pltpu

F32 = jnp.float32
BF16 = jnp.bfloat16

RET_HEADS = 4
CONV_WIDTH = 3
ROPE_BASE = 10000.0
NORM_EPS = 1e-6
GN_EPS = 1e-6
PAST_LEN = 4096

LANES = 128
SUBLANES = 8
VMEM_LIMIT_BYTES = 56 * 1024 * 1024

SEQ_TILE = 512
RET_SUB_TILE = 512
CONV_TILE = 1024
SAMPLE_STREAMS = 4
FFN_TILE = 1024
FFN_TAIL_SPLIT = 4
RET_CHUNK = 256
ROPE_TILE = 512


def _const_spec(shape):
    nd = len(shape)
    return pl.BlockSpec(shape, lambda *_: (0,) * nd, pipeline_mode=pl.Buffered(1))


def _layer_spec(block, layer, col=0):
    return pl.BlockSpec((None,) + tuple(block), lambda *_: (layer, 0, col), pipeline_mode=pl.Buffered(1))


def _rms(x, g):
    ms = jnp.mean(x * x, axis=-1, keepdims=True)
    return (x * lax.rsqrt(ms + NORM_EPS)) * g


def _dot(a, b):
    return jnp.dot(a, b, preferred_element_type=F32)


def _rope_table_kernel(inv_ref, sign_ref, cos_ref, sin_ref, cos_row_scr, sin_row_scr):
    rows = cos_ref.shape[0]
    tile = pl.program_id(0)

    @pl.when(tile == 0)
    def _rows_once():
        b = lax.broadcasted_iota(jnp.int32, cos_ref.shape, 0).astype(F32) * inv_ref[...]
        cos_row_scr[...] = jnp.cos(b)
        sin_row_scr[...] = jnp.sin(b)

    a = (tile * rows).astype(F32) * inv_ref[...]
    ca, sa = jnp.cos(a), jnp.sin(a)
    cb, sb = cos_row_scr[...], sin_row_scr[...]
    cos_ref[...] = ca * cb - sa * sb
    sin_ref[...] = (sa * cb + ca * sb) * sign_ref[...]


def _rope_tables(n_pos, dk):
    inv = 1.0 / (ROPE_BASE ** jnp.linspace(0.0, 1.0, dk // 2, dtype=F32))
    inv = jnp.repeat(inv, 2).reshape(1, dk)
    sign = jnp.tile(jnp.array([-1.0, 1.0], F32), dk // 2).reshape(1, dk)
    out = jax.ShapeDtypeStruct((n_pos, dk), F32)
    return pl.pallas_call(
        _rope_table_kernel,
        grid=(n_pos // ROPE_TILE,),
        in_specs=[pl.BlockSpec((1, dk), lambda i: (0, 0))] * 2,
        out_specs=[pl.BlockSpec((ROPE_TILE, dk), lambda i: (i, 0))] * 2,
        out_shape=[out, out],
        scratch_shapes=[pltpu.VMEM((ROPE_TILE, dk), F32)] * 2,
        compiler_params=pltpu.CompilerParams(dimension_semantics=("arbitrary",)),
        name="rope_tables",
    )(inv, sign)


def _log_gamma(h):
    return float(np.log(1.0 - 2.0 ** (-5.0 - h)))


def _decay_table_kernel(intra_ref, qdec_ref, kdec_ref):
    chunk = intra_ref.shape[1]
    diff = (lax.broadcasted_iota(jnp.int32, (chunk, chunk), 0)
            - lax.broadcasted_iota(jnp.int32, (chunk, chunk), 1))
    row = lax.broadcasted_iota(jnp.int32, (chunk, LANES), 0).astype(F32)
    for h in range(RET_HEADS):
        lg = _log_gamma(h)
        intra_ref[h] = jnp.where(diff >= 0, jnp.exp(jnp.maximum(diff, 0).astype(F32) * lg), 0.0)
        qdec_ref[h] = jnp.exp((row + 1.0) * lg)
        kdec_ref[h] = jnp.exp((chunk - 1.0 - row) * lg)


def _decay_tables(chunk):
    return pl.pallas_call(
        _decay_table_kernel,
        out_shape=[jax.ShapeDtypeStruct((RET_HEADS, chunk, chunk), F32),
                   jax.ShapeDtypeStruct((RET_HEADS, chunk, LANES), F32),
                   jax.ShapeDtypeStruct((RET_HEADS, chunk, LANES), F32)],
        name="decay_tables",
    )()


def _ret_kernel(*refs, nb, tt, st, chunk, has_state, has_prev, single_tile):
    refs = list(refs)
    (x_ref, cos_ref, sin_ref, intra_ref, qdec_ref, kdec_ref, gmix_ref, wq_ref, wk_ref, wv_ref, wg_ref,
     gng_ref, gnb_ref, wo_ref) = refs[:14]
    del refs[:14]
    s0_ref = refs.pop(0) if has_state else None
    if has_prev:
        refs.pop(0)
    o_ref, sout_ref, gated_scr = refs
    d = x_ref.shape[-1]
    dk = wq_ref.shape[1] // RET_HEADS
    dv = wv_ref.shape[1] // RET_HEADS
    t = pl.program_id(1)

    if not single_tile:
        @pl.when(t == 0)
        def _init_state():
            sout_ref[...] = s0_ref[...] if has_state else jnp.zeros_like(sout_ref)

    def state(b, h, first_chunk):
        if single_tile and first_chunk:
            return s0_ref[b, h] if has_state else jnp.zeros((dk, dv), F32)
        return sout_ref[b, h]

    even_lane = (lax.broadcasted_iota(jnp.int32, (nb * st, LANES), 1) & 1) == 0
    groups = dk // LANES
    n_chunks = st // chunk
    blocks = [(b, c) for b in range(nb) for c in range(n_chunks)]

    def rows(b, c):
        r0 = b * st + c * chunk
        return slice(r0, r0 + chunk)

    def scale_rows(z, col):
        return jnp.concatenate([z[:, j * LANES:(j + 1) * LANES] * col for j in range(z.shape[1] // LANES)],
                               axis=1)

    for s in range(tt // st):
        x = x_ref[:, s * st:(s + 1) * st, :].reshape(nb * st, d)
        xn = _rms(x, gmix_ref[...]).astype(BF16)
        cos = cos_ref[s * st:(s + 1) * st, :]
        sin = sin_ref[s * st:(s + 1) * st, :]
        if nb > 1:
            cos = jnp.concatenate([cos] * nb, axis=0)
            sin = jnp.concatenate([sin] * nb, axis=0)

        def rotate(z):
            parts = []
            for j in range(z.shape[1] // LANES):
                zj = z[:, j * LANES:(j + 1) * LANES]
                tj = j % groups
                partner = jnp.where(even_lane, pltpu.roll(zj, LANES - 1, 1), pltpu.roll(zj, 1, 1))
                parts.append(zj * cos[:, tj * LANES:(tj + 1) * LANES]
                             + partner * sin[:, tj * LANES:(tj + 1) * LANES])
            return jnp.concatenate(parts, axis=1)

        q = rotate(_dot(xn, wq_ref[...])).astype(BF16)
        k = rotate(_dot(xn, wk_ref[...])) * (dk ** -0.5)
        v = _dot(xn, wv_ref[...]).astype(BF16)
        kb = k.astype(BF16)

        scores = {}
        for (b, c) in blocks:
            for h in range(RET_HEADS):
                sc = lax.dot_general(q[rows(b, c), h * dk:(h + 1) * dk], kb[rows(b, c), h * dk:(h + 1) * dk],
                                     (((1,), (1,)), ((), ())), preferred_element_type=F32)
                scores[b, c, h] = (sc * intra_ref[h]).astype(BF16)

        outs = {}
        for (b, c) in blocks:
            first = s == 0 and c == 0
            for h in range(RET_HEADS):
                qc = q[rows(b, c), h * dk:(h + 1) * dk]
                vc = v[rows(b, c), h * dv:(h + 1) * dv]
                outs[b, c, h] = (_dot(scores[b, c, h], vc)
                                 + scale_rows(_dot(qc, state(b, h, first).astype(BF16)), qdec_ref[h]))
            for h in range(RET_HEADS):
                kd = scale_rows(k[rows(b, c), h * dk:(h + 1) * dk], kdec_ref[h]).astype(BF16)
                vc = v[rows(b, c), h * dv:(h + 1) * dv]
                sout_ref[b, h] = (state(b, h, first) * float(np.exp(chunk * _log_gamma(h)))
                                  + lax.dot_general(kd, vc, (((0,), (0,)), ((), ())),
                                                    preferred_element_type=F32))

        for h in range(RET_HEADS):
            g = _dot(xn, wg_ref[:, h * dv:(h + 1) * dv])
            gn_g = gng_ref[:, h * dv:(h + 1) * dv]
            gn_b = gnb_ref[:, h * dv:(h + 1) * dv]
            for b in range(nb):
                chunks = [outs[b, c, h] for c in range(n_chunks)]
                o = chunks[0] if n_chunks == 1 else jnp.concatenate(chunks, axis=0)
                mu = jnp.mean(o, axis=-1, keepdims=True)
                oc = o - mu
                var = jnp.mean(oc * oc, axis=-1, keepdims=True)
                on = (oc * lax.rsqrt(var + GN_EPS)) * gn_g + gn_b
                gb = g[b * st:(b + 1) * st]
                gated = (gb * jax.nn.sigmoid(gb)) * on
                gated_scr[b * st:(b + 1) * st, h * dv:(h + 1) * dv] = gated.astype(BF16)

        y = _dot(gated_scr[...], wo_ref[...])
        o_ref[:, s * st:(s + 1) * st, :] = (x + y).reshape(nb, st, d)


def _ret_mixer(x, cos, sin, decay, pos0, wts, mix_layer, states_in, states_out, layer, *, nb, tt):
    bsz, seq, d = x.shape
    w_in, wo = wts["w_ret_in"], wts["w_ret_out"]
    n_layers, vw = wo.shape[0], wo.shape[1]
    qkw = (w_in.shape[2] - 2 * vw) // 2
    dk, dv = qkw // RET_HEADS, vw // RET_HEADS
    chunk = decay[0].shape[1]
    st = min(tt, RET_SUB_TILE)
    assert pos0 % tt == 0 and seq % tt == 0 and tt % st == 0 and st % chunk == 0 and bsz % nb == 0
    assert (2 * qkw) % vw == 0
    v_col = 2 * qkw // vw
    has_state = states_in is not None
    has_prev = states_out is not None
    pos_blk = pos0 // tt
    in_specs = [
        pl.BlockSpec((nb, tt, d), lambda b, t: (b, t, 0)),
        pl.BlockSpec((tt, dk), lambda b, t: (pos_blk + t, 0)),
        pl.BlockSpec((tt, dk), lambda b, t: (pos_blk + t, 0)),
        _const_spec((RET_HEADS, chunk, chunk)),
        _const_spec((RET_HEADS, chunk, LANES)), _const_spec((RET_HEADS, chunk, LANES)),
        _layer_spec((1, d), mix_layer),
        _layer_spec((d, qkw), layer, 0), _layer_spec((d, qkw), layer, 1),
        _layer_spec((d, vw), layer, v_col), _layer_spec((d, vw), layer, v_col + 1),
        _layer_spec((1, vw), layer), _layer_spec((1, vw), layer),
        _layer_spec((vw, d), layer),
    ]
    args = [x, cos, sin, *decay, wts["g_mix"], w_in, w_in, w_in, w_in, wts["ret_gn_g"], wts["ret_gn_b"], wo]
    state_spec = pl.BlockSpec((None, nb, RET_HEADS, dk, dv), lambda b, t: (layer, b, 0, 0, 0))
    aliases = {}
    if has_state:
        in_specs.append(state_spec)
        args.append(states_in)
    if has_prev:
        aliases[len(args)] = 1
        in_specs.append(pl.BlockSpec(memory_space=pl.ANY))
        args.append(states_out)
    return pl.pallas_call(
        functools.partial(_ret_kernel, nb=nb, tt=tt, st=st, chunk=chunk, has_state=has_state,
                          has_prev=has_prev, single_tile=(seq == tt)),
        grid=(bsz // nb, seq // tt),
        in_specs=in_specs,
        out_specs=[pl.BlockSpec((nb, tt, d), lambda b, t: (b, t, 0)), state_spec],
        out_shape=[jax.ShapeDtypeStruct(x.shape, x.dtype),
                   jax.ShapeDtypeStruct((n_layers, bsz, RET_HEADS, dk, dv), x.dtype)],
        scratch_shapes=[pltpu.VMEM((nb * st, vw), BF16)],
        input_output_aliases=aliases,
        compiler_params=pltpu.CompilerParams(dimension_semantics=("parallel", "arbitrary"),
                                             vmem_limit_bytes=VMEM_LIMIT_BYTES),
        name="ret_mixer_state" if has_state else "ret_mixer",
    )(*args)


def _conv_kernel(*refs, nb, tt, has_state, has_prev):
    refs = list(refs)
    x_ref, gmix_ref, win_ref, cw_ref, wo_ref = refs[:5]
    del refs[:5]
    h0_ref = refs.pop(0) if has_state else None
    if has_prev:
        refs.pop(0)
    o_ref, hout_ref, u_scr = refs
    d = x_ref.shape[-1]
    nh = CONV_WIDTH - 1
    head = SUBLANES
    t = pl.program_id(1)

    @pl.when(t == 0)
    def _load_hist():
        if has_state:
            u_scr[:, head - nh:head, :] = h0_ref[...]
        else:
            u_scr[:, head - nh:head, :] = jnp.zeros((nb, nh, d), F32)

    x = x_ref[...].reshape(nb * tt, d)
    xn = _rms(x, gmix_ref[...]).astype(BF16)
    u = _dot(xn, win_ref[:, d:2 * d]) * _dot(xn, win_ref[:, 2 * d:3 * d])
    u_scr[:, head:head + tt, :] = u.reshape(nb, tt, d)
    bg = _dot(xn, win_ref[:, 0:d])
    y = cw_ref[CONV_WIDTH - 1:CONV_WIDTH, :] * u
    for j in range(CONV_WIDTH - 1):
        off = head - nh + j
        y = y + cw_ref[j:j + 1, :] * u_scr[:, off:off + tt, :].reshape(nb * tt, d)
    z = _dot((bg * y).astype(BF16), wo_ref[...])
    o_ref[...] = (x + z).reshape(nb, tt, d)
    tail = u_scr[:, head + tt - nh:head + tt, :]
    u_scr[:, head - nh:head, :] = tail
    hout_ref[...] = tail


def _conv_mixer(x, wts, mix_layer, hist_in, hist_out, layer, *, nb, tt):
    bsz, seq, d = x.shape
    nh = CONV_WIDTH - 1
    n_layers = wts["w_conv_in"].shape[0]
    assert seq % tt == 0 and bsz % nb == 0 and tt >= nh
    has_state = hist_in is not None
    has_prev = hist_out is not None
    in_specs = [
        pl.BlockSpec((nb, tt, d), lambda b, t: (b, t, 0)),
        _layer_spec((1, d), mix_layer),
        _layer_spec((d, 3 * d), layer),
        _layer_spec((CONV_WIDTH, d), layer),
        _layer_spec((d, d), layer),
    ]
    args = [x, wts["g_mix"], wts["w_conv_in"], wts["conv_w"], wts["w_conv_out"]]
    hist_spec = pl.BlockSpec((None, nb, nh, d), lambda b, t: (layer, b, 0, 0))
    aliases = {}
    if has_state:
        in_specs.append(hist_spec)
        args.append(hist_in)
    if has_prev:
        aliases[len(args)] = 1
        in_specs.append(pl.BlockSpec(memory_space=pl.ANY))
        args.append(hist_out)
    return pl.pallas_call(
        functools.partial(_conv_kernel, nb=nb, tt=tt, has_state=has_state, has_prev=has_prev),
        grid=(bsz // nb, seq // tt),
        in_specs=in_specs,
        out_specs=[pl.BlockSpec((nb, tt, d), lambda b, t: (b, t, 0)), hist_spec],
        out_shape=[jax.ShapeDtypeStruct(x.shape, x.dtype),
                   jax.ShapeDtypeStruct((n_layers, bsz, nh, d), x.dtype)],
        scratch_shapes=[pltpu.VMEM((nb, SUBLANES + tt, d), F32)],
        input_output_aliases=aliases,
        compiler_params=pltpu.CompilerParams(dimension_semantics=("parallel", "arbitrary"),
                                             vmem_limit_bytes=VMEM_LIMIT_BYTES),
        name="conv_mixer_state" if has_state else "conv_mixer",
    )(*args)


def _ffn_kernel(x_ref, p_ref, gmlp_ref, wup_ref, wdn_ref, gple_ref, wgate_ref, wproj_ref, gfin_ref,
                o_ref, u_scr, *, final):
    d = x_ref.shape[-1]
    dff = wup_ref.shape[1]
    x = x_ref[...]
    xn = _rms(x, gmlp_ref[...]).astype(BF16)
    for j in range(dff // d):
        u = jnp.maximum(_dot(xn, wup_ref[:, j * d:(j + 1) * d]), 0.0)
        u_scr[:, j * d:(j + 1) * d] = (u * u).astype(BF16)
    h1 = x + _dot(u_scr[...], wdn_ref[...])
    rows = x.shape[0] // FFN_TAIL_SPLIT
    for i in range(FFN_TAIL_SPLIT):
        rs = slice(i * rows, (i + 1) * rows)
        pp = _dot(p_ref[rs, :].astype(BF16), wproj_ref[...])
        gate = jax.nn.sigmoid(_dot(_rms(h1[rs], gple_ref[...]).astype(BF16), wgate_ref[...]))
        h2 = h1[rs] + pp * gate
        if final:
            h2 = _rms(h2, gfin_ref[...])
        o_ref[rs, :] = h2


def _ffn_ple(x, p, layer, wts, *, tile, final):
    n, d = x.shape
    dff = wts["w_up"].shape[2]
    pd = p.shape[-1]
    tile = min(tile, n)
    assert n % tile == 0
    return pl.pallas_call(
        functools.partial(_ffn_kernel, final=final),
        grid=(n // tile,),
        in_specs=[
            pl.BlockSpec((tile, d), lambda i: (i, 0)),
            pl.BlockSpec((None, tile, pd), lambda i: (layer, i, 0)),
            _layer_spec((1, d), layer),
            _layer_spec((d, dff), layer), _layer_spec((dff, d), layer),
            _layer_spec((1, d), layer),
            _layer_spec((d, d), layer), _layer_spec((pd, d), layer),
            _const_spec((1, d)),
        ],
        out_specs=pl.BlockSpec((tile, d), lambda i: (i, 0)),
        out_shape=jax.ShapeDtypeStruct(x.shape, x.dtype),
        scratch_shapes=[pltpu.VMEM((tile, dff), BF16)],
        compiler_params=pltpu.CompilerParams(dimension_semantics=("parallel",),
                                             vmem_limit_bytes=VMEM_LIMIT_BYTES),
        name="ffn_ple_final" if final else "ffn_ple",
    )(x, p, wts["g_mlp"], wts["w_up"], wts["w_down"], wts["g_ple"], wts["w_ple_gate"], wts["w_ple_proj"],
      wts["g_final"])


def _trunk(x, p, ret_states, conv_states, pos0, cos, sin, wts, *, nb, tt, conv_tt, ffn_tile):
    bsz, seq, d = x.shape
    depth = p.shape[0]
    p = p.reshape(depth, bsz * seq, p.shape[-1])
    decay = _decay_tables(min(tt, RET_CHUNK))
    new_ret = new_conv = None
    h = x
    for i in range(depth):
        r = i // 2
        if i % 2 == 0:
            h, new_ret = _ret_mixer(h, cos, sin, decay, pos0, wts, i, ret_states, new_ret, r, nb=nb, tt=tt)
        else:
            h, new_conv = _conv_mixer(h, wts, i, conv_states, new_conv, r, nb=nb, tt=conv_tt)
        h = _ffn_ple(h.reshape(bsz * seq, d), p, i, wts, tile=ffn_tile,
                     final=(i == depth - 1)).reshape(bsz, seq, d)
    return h, new_ret, new_conv


def kernel(x_prompt, x_sample, state_ret, state_conv, p_prompt, p_sample, g_mix, w_ret_in, ret_gn_g, ret_gn_b, w_ret_out, w_conv_in, conv_w, w_conv_out, g_mlp, w_up, w_down, g_ple, w_ple_gate, w_ple_proj, g_final):
    depth = p_prompt.shape[0]
    seq, dec_seq = x_prompt.shape[1], x_sample.shape[1]
    vw = w_ret_out.shape[1]
    qkw = (w_ret_in.shape[2] - 2 * vw) // 2
    dk = qkw // RET_HEADS

    rows = lambda a: a.reshape(a.shape[0], 1, a.shape[1])
    wts = {
        "g_mix": rows(g_mix), "g_mlp": rows(g_mlp), "g_ple": rows(g_ple), "g_final": g_final.reshape(1, -1),
        "ret_gn_g": rows(ret_gn_g), "ret_gn_b": rows(ret_gn_b), "conv_w": conv_w,
        "w_ret_in": w_ret_in.astype(BF16), "w_ret_out": w_ret_out.astype(BF16),
        "w_conv_in": w_conv_in.astype(BF16), "w_conv_out": w_conv_out.astype(BF16),
        "w_up": w_up.astype(BF16), "w_down": w_down.astype(BF16),
        "w_ple_gate": w_ple_gate.astype(BF16), "w_ple_proj": w_ple_proj.astype(BF16),
    }

    n_pos = max(seq, PAST_LEN + dec_seq)
    n_pos = -(-n_pos // ROPE_TILE) * ROPE_TILE
    cos, sin = _rope_tables(n_pos, dk)

    y_p, ret_p, conv_p = _trunk(x_prompt, p_prompt, None, None, 0, cos, sin, wts,
                                nb=1, tt=SEQ_TILE, conv_tt=CONV_TILE, ffn_tile=FFN_TILE)
    y_s, ret_s, conv_s = _trunk(x_sample, p_sample, state_ret, state_conv, PAST_LEN, cos, sin, wts,
                                nb=SAMPLE_STREAMS, tt=dec_seq, conv_tt=dec_seq, ffn_tile=FFN_TILE)
    return (y_p, y_s, ret_p, conv_p, ret_s, conv_s)
```

```python
import functools

import numpy as np
import jax
import jax.numpy as jnp
from jax import lax
from jax.experimental import pallas as pl
from jax.experimental.pallas import tpu as pltpu

F32 = jnp.float32
BF16 = jnp.bfloat16

RET_HEADS = 4
CONV_WIDTH = 3
ROPE_BASE = 10000.0
NORM_EPS = 1e-6
GN_EPS = 1e-6
PAST_LEN = 4096

LANES = 128
SUBLANES = 8
VMEM_LIMIT_BYTES = 56 * 1024 * 1024

SEQ_TILE = 512
CONV_TILE = 1024
SAMPLE_STREAMS = 4
FFN_TILE = 1024
FFN_TAIL_SPLIT = 4
RET_CHUNK = 256
ROPE_TILE = 512


def _const_spec(shape):
    nd = len(shape)
    return pl.BlockSpec(shape, lambda *_: (0,) * nd, pipeline_mode=pl.Buffered(1))


def _layer_spec(block, layer, col=0):
    return pl.BlockSpec((None,) + tuple(block), lambda *_: (layer, 0, col), pipeline_mode=pl.Buffered(1))


def _rms(x, g):
    ms = jnp.mean(x * x, axis=-1, keepdims=True)
    return (x * lax.rsqrt(ms + NORM_EPS)) * g


def _dot(a, b):
    return jnp.dot(a, b, preferred_element_type=F32)


def _rope_table_kernel(inv_ref, sign_ref, cos_ref, sin_ref, cos_row_scr, sin_row_scr):
    rows = cos_ref.shape[0]
    tile = pl.program_id(0)

    @pl.when(tile == 0)
    def _rows_once():
        b = lax.broadcasted_iota(jnp.int32, cos_ref.shape, 0).astype(F32) * inv_ref[...]
        cos_row_scr[...] = jnp.cos(b)
        sin_row_scr[...] = jnp.sin(b)

    a = (tile * rows).astype(F32) * inv_ref[...]
    ca, sa = jnp.cos(a), jnp.sin(a)
    cb, sb = cos_row_scr[...], sin_row_scr[...]
    cos_ref[...] = ca * cb - sa * sb
    sin_ref[...] = (sa * cb + ca * sb) * sign_ref[...]


def _rope_tables(n_pos, dk):
    inv = 1.0 / (ROPE_BASE ** jnp.linspace(0.0, 1.0, dk // 2, dtype=F32))
    inv = jnp.repeat(inv, 2).reshape(1, dk)
    sign = jnp.tile(jnp.array([-1.0, 1.0], F32), dk // 2).reshape(1, dk)
    out = jax.ShapeDtypeStruct((n_pos, dk), F32)
    return pl.pallas_call(
        _rope_table_kernel,
        grid=(n_pos // ROPE_TILE,),
        in_specs=[pl.BlockSpec((1, dk), lambda i: (0, 0))] * 2,
        out_specs=[pl.BlockSpec((ROPE_TILE, dk), lambda i: (i, 0))] * 2,
        out_shape=[out, out],
        scratch_shapes=[pltpu.VMEM((ROPE_TILE, dk), F32)] * 2,
        compiler_params=pltpu.CompilerParams(dimension_semantics=("arbitrary",)),
        name="rope_tables",
    )(inv, sign)


def _log_gamma(h):
    return float(np.log(1.0 - 2.0 ** (-5.0 - h)))


def _decay_table_kernel(intra_ref, qdec_ref, kdec_ref):
    chunk = intra_ref.shape[1]
    diff = (lax.broadcasted_iota(jnp.int32, (chunk, chunk), 0)
            - lax.broadcasted_iota(jnp.int32, (chunk, chunk), 1))
    row = lax.broadcasted_iota(jnp.int32, (chunk, LANES), 0).astype(F32)
    for h in range(RET_HEADS):
        lg = _log_gamma(h)
        intra_ref[h] = jnp.where(diff >= 0, jnp.exp(jnp.maximum(diff, 0).astype(F32) * lg), 0.0)
        qdec_ref[h] = jnp.exp((row + 1.0) * lg)
        kdec_ref[h] = jnp.exp((chunk - 1.0 - row) * lg)


def _decay_tables(chunk):
    return pl.pallas_call(
        _decay_table_kernel,
        out_shape=[jax.ShapeDtypeStruct((RET_HEADS, chunk, chunk), F32),
                   jax.ShapeDtypeStruct((RET_HEADS, chunk, LANES), F32),
                   jax.ShapeDtypeStruct((RET_HEADS, chunk, LANES), F32)],
        name="decay_tables",
    )()


def _ret_kernel(*refs, nb, tt, chunk, has_state, has_prev, single_tile):
    refs = list(refs)
    (x_ref, cos_ref, sin_ref, intra_ref, qdec_ref, kdec_ref, gmix_ref, wq_ref, wk_ref, wv_ref, wg_ref,
     gng_ref, gnb_ref, wo_ref) = refs[:14]
    del refs[:14]
    s0_ref = refs.pop(0) if has_state else None
    if has_prev:
        refs.pop(0)
    o_ref, sout_ref, gated_scr = refs
    d = x_ref.shape[-1]
    dk = wq_ref.shape[1] // RET_HEADS
    dv = wv_ref.shape[1] // RET_HEADS
    t = pl.program_id(1)

    if not single_tile:
        @pl.when(t == 0)
        def _init_state():
            sout_ref[...] = s0_ref[...] if has_state else jnp.zeros_like(sout_ref)

    def state(b, h, first_chunk):
        if single_tile and first_chunk:
            return s0_ref[b, h] if has_state else jnp.zeros((dk, dv), F32)
        return sout_ref[b, h]

    even_lane = (lax.broadcasted_iota(jnp.int32, (nb * tt, LANES), 1) & 1) == 0
    groups = dk // LANES
    n_chunks = tt // chunk
    blocks = [(b, c) for b in range(nb) for c in range(n_chunks)]

    def rows(b, c):
        r0 = b * tt + c * chunk
        return slice(r0, r0 + chunk)

    def scale_rows(z, col):
        return jnp.concatenate([z[:, j * LANES:(j + 1) * LANES] * col for j in range(z.shape[1] // LANES)],
                               axis=1)

    x = x_ref[...].reshape(nb * tt, d)
    xn = _rms(x, gmix_ref[...]).astype(BF16)
    cos = cos_ref[...]
    sin = sin_ref[...]
    if nb > 1:
        cos = jnp.concatenate([cos] * nb, axis=0)
        sin = jnp.concatenate([sin] * nb, axis=0)

    def rotate(z):
        parts = []
        for j in range(z.shape[1] // LANES):
            zj = z[:, j * LANES:(j + 1) * LANES]
            tj = j % groups
            partner = jnp.where(even_lane, pltpu.roll(zj, LANES - 1, 1), pltpu.roll(zj, 1, 1))
            parts.append(zj * cos[:, tj * LANES:(tj + 1) * LANES]
                         + partner * sin[:, tj * LANES:(tj + 1) * LANES])
        return jnp.concatenate(parts, axis=1)

    q = rotate(_dot(xn, wq_ref[...])).astype(BF16)
    k = rotate(_dot(xn, wk_ref[...])) * (dk ** -0.5)
    v = _dot(xn, wv_ref[...]).astype(BF16)
    kb = k.astype(BF16)

    scores = {}
    for (b, c) in blocks:
        for h in range(RET_HEADS):
            sc = lax.dot_general(q[rows(b, c), h * dk:(h + 1) * dk], kb[rows(b, c), h * dk:(h + 1) * dk],
                                 (((1,), (1,)), ((), ())), preferred_element_type=F32)
            scores[b, c, h] = (sc * intra_ref[h]).astype(BF16)

    outs = {}
    for (b, c) in blocks:
        first = c == 0
        for h in range(RET_HEADS):
            qc = q[rows(b, c), h * dk:(h + 1) * dk]
            vc = v[rows(b, c), h * dv:(h + 1) * dv]
            outs[b, c, h] = (_dot(scores[b, c, h], vc)
                             + scale_rows(_dot(qc, state(b, h, first).astype(BF16)), qdec_ref[h]))
        for h in range(RET_HEADS):
            kd = scale_rows(k[rows(b, c), h * dk:(h + 1) * dk], kdec_ref[h]).astype(BF16)
            vc = v[rows(b, c), h * dv:(h + 1) * dv]
            sout_ref[b, h] = (state(b, h, first) * float(np.exp(chunk * _log_gamma(h)))
                              + lax.dot_general(kd, vc, (((0,), (0,)), ((), ())),
                                                preferred_element_type=F32))

    for h in range(RET_HEADS):
        g = _dot(xn, wg_ref[:, h * dv:(h + 1) * dv])
        gn_g = gng_ref[:, h * dv:(h + 1) * dv]
        gn_b = gnb_ref[:, h * dv:(h + 1) * dv]
        for b in range(nb):
            chunks = [outs[b, c, h] for c in range(n_chunks)]
            o = chunks[0] if n_chunks == 1 else jnp.concatenate(chunks, axis=0)
            mu = jnp.mean(o, axis=-1, keepdims=True)
            oc = o - mu
            var = jnp.mean(oc * oc, axis=-1, keepdims=True)
            on = (oc * lax.rsqrt(var + GN_EPS)) * gn_g + gn_b
            gb = g[b * tt:(b + 1) * tt]
            gated = (gb * jax.nn.sigmoid(gb)) * on
            gated_scr[b * tt:(b + 1) * tt, h * dv:(h + 1) * dv] = gated.astype(BF16)

    y = _dot(gated_scr[...], wo_ref[...])
    o_ref[...] = (x + y).reshape(nb, tt, d)


def _ret_mixer(x, cos, sin, decay, pos0, wts, mix_layer, states_in, states_out, layer, *, nb, tt):
    bsz, seq, d = x.shape
    w_in, wo = wts["w_ret_in"], wts["w_ret_out"]
    n_layers, vw = wo.shape[0], wo.shape[1]
    qkw = (w_in.shape[2] - 2 * vw) // 2
    dk, dv = qkw // RET_HEADS, vw // RET_HEADS
    chunk = decay[0].shape[1]
    assert pos0 % tt == 0 and seq % tt == 0 and tt % chunk == 0 and bsz % nb == 0
    assert (2 * qkw) % vw == 0
    v_col = 2 * qkw // vw
    has_state = states_in is not None
    has_prev = states_out is not None
    pos_blk = pos0 // tt
    in_specs = [
        pl.BlockSpec((nb, tt, d), lambda b, t: (b, t, 0)),
        pl.BlockSpec((tt, dk), lambda b, t: (pos_blk + t, 0)),
        pl.BlockSpec((tt, dk), lambda b, t: (pos_blk + t, 0)),
        _const_spec((RET_HEADS, chunk, chunk)),
        _const_spec((RET_HEADS, chunk, LANES)), _const_spec((RET_HEADS, chunk, LANES)),
        _layer_spec((1, d), mix_layer),
        _layer_spec((d, qkw), layer, 0), _layer_spec((d, qkw), layer, 1),
        _layer_spec((d, vw), layer, v_col), _layer_spec((d, vw), layer, v_col + 1),
        _layer_spec((1, vw), layer), _layer_spec((1, vw), layer),
        _layer_spec((vw, d), layer),
    ]
    args = [x, cos, sin, *decay, wts["g_mix"], w_in, w_in, w_in, w_in, wts["ret_gn_g"], wts["ret_gn_b"], wo]
    state_spec = pl.BlockSpec((None, nb, RET_HEADS, dk, dv), lambda b, t: (layer, b, 0, 0, 0))
    aliases = {}
    if has_state:
        in_specs.append(state_spec)
        args.append(states_in)
    if has_prev:
        aliases[len(args)] = 1
        in_specs.append(pl.BlockSpec(memory_space=pl.ANY))
        args.append(states_out)
    return pl.pallas_call(
        functools.partial(_ret_kernel, nb=nb, tt=tt, chunk=chunk, has_state=has_state, has_prev=has_prev,
                          single_tile=(seq == tt)),
        grid=(bsz // nb, seq // tt),
        in_specs=in_specs,
        out_specs=[pl.BlockSpec((nb, tt, d), lambda b, t: (b, t, 0)), state_spec],
        out_shape=[jax.ShapeDtypeStruct(x.shape, x.dtype),
                   jax.ShapeDtypeStruct((n_layers, bsz, RET_HEADS, dk, dv), x.dtype)],
        scratch_shapes=[pltpu.VMEM((nb * tt, vw), BF16)],
        input_output_aliases=aliases,
        compiler_params=pltpu.CompilerParams(dimension_semantics=("parallel", "arbitrary"),
                                             vmem_limit_bytes=VMEM_LIMIT_BYTES),
        name="ret_mixer_state" if has_state else "ret_mixer",
    )(*args)


def _conv_kernel(*refs, nb, tt, has_state, has_prev):
    refs = list(refs)
    x_ref, gmix_ref, win_ref, cw_ref, wo_ref = refs[:5]
    del refs[:5]
    h0_ref = refs.pop(0) if has_state else None
    if has_prev:
        refs.pop(0)
    o_ref, hout_ref, u_scr = refs
    d = x_ref.shape[-1]
    nh = CONV_WIDTH - 1
    head = SUBLANES
    t = pl.program_id(1)

    @pl.when(t == 0)
    def _load_hist():
        if has_state:
            u_scr[:, head - nh:head, :] = h0_ref[...]
        else:
            u_scr[:, head - nh:head, :] = jnp.zeros((nb, nh, d), F32)

    x = x_ref[...].reshape(nb * tt, d)
    xn = _rms(x, gmix_ref[...]).astype(BF16)
    u = _dot(xn, win_ref[:, d:2 * d]) * _dot(xn, win_ref[:, 2 * d:3 * d])
    u_scr[:, head:head + tt, :] = u.reshape(nb, tt, d)
    bg = _dot(xn, win_ref[:, 0:d])
    y = cw_ref[CONV_WIDTH - 1:CONV_WIDTH, :] * u
    for j in range(CONV_WIDTH - 1):
        off = head - nh + j
        y = y + cw_ref[j:j + 1, :] * u_scr[:, off:off + tt, :].reshape(nb * tt, d)
    z = _dot((bg * y).astype(BF16), wo_ref[...])
    o_ref[...] = (x + z).reshape(nb, tt, d)
    tail = u_scr[:, head + tt - nh:head + tt, :]
    u_scr[:, head - nh:head, :] = tail
    hout_ref[...] = tail


def _conv_mixer(x, wts, mix_layer, hist_in, hist_out, layer, *, nb, tt):
    bsz, seq, d = x.shape
    nh = CONV_WIDTH - 1
    n_layers = wts["w_conv_in"].shape[0]
    assert seq % tt == 0 and bsz % nb == 0 and tt >= nh
    has_state = hist_in is not None
    has_prev = hist_out is not None
    in_specs = [
        pl.BlockSpec((nb, tt, d), lambda b, t: (b, t, 0)),
        _layer_spec((1, d), mix_layer),
        _layer_spec((d, 3 * d), layer),
        _layer_spec((CONV_WIDTH, d), layer),
        _layer_spec((d, d), layer),
    ]
    args = [x, wts["g_mix"], wts["w_conv_in"], wts["conv_w"], wts["w_conv_out"]]
    hist_spec = pl.BlockSpec((None, nb, nh, d), lambda b, t: (layer, b, 0, 0))
    aliases = {}
    if has_state:
        in_specs.append(hist_spec)
        args.append(hist_in)
    if has_prev:
        aliases[len(args)] = 1
        in_specs.append(pl.BlockSpec(memory_space=pl.ANY))
        args.append(hist_out)
    return pl.pallas_call(
        functools.partial(_conv_kernel, nb=nb, tt=tt, has_state=has_state, has_prev=has_prev),
        grid=(bsz // nb, seq // tt),
        in_specs=in_specs,
        out_specs=[pl.BlockSpec((nb, tt, d), lambda b, t: (b, t, 0)), hist_spec],
        out_shape=[jax.ShapeDtypeStruct(x.shape, x.dtype),
                   jax.ShapeDtypeStruct((n_layers, bsz, nh, d), x.dtype)],
        scratch_shapes=[pltpu.VMEM((nb, SUBLANES + tt, d), F32)],
        input_output_aliases=aliases,
        compiler_params=pltpu.CompilerParams(dimension_semantics=("parallel", "arbitrary"),
                                             vmem_limit_bytes=VMEM_LIMIT_BYTES),
        name="conv_mixer_state" if has_state else "conv_mixer",
    )(*args)


def _ffn_kernel(x_ref, p_ref, gmlp_ref, wup_ref, wdn_ref, gple_ref, wgate_ref, wproj_ref, gfin_ref,
                o_ref, u_scr, *, final):
    d = x_ref.shape[-1]
    dff = wup_ref.shape[1]
    x = x_ref[...]
    xn = _rms(x, gmlp_ref[...]).astype(BF16)
    for j in range(dff // d):
        u = jnp.maximum(_dot(xn, wup_ref[:, j * d:(j + 1) * d]), 0.0)
        u_scr[:, j * d:(j + 1) * d] = (u * u).astype(BF16)
    h1 = x + _dot(u_scr[...], wdn_ref[...])
    rows = x.shape[0] // FFN_TAIL_SPLIT
    for i in range(FFN_TAIL_SPLIT):
        rs = slice(i * rows, (i + 1) * rows)
        pp = _dot(p_ref[rs, :].astype(BF16), wproj_ref[...])
        gate = jax.nn.sigmoid(_dot(_rms(h1[rs], gple_ref[...]).astype(BF16), wgate_ref[...]))
        h2 = h1[rs] + pp * gate
        if final:
            h2 = _rms(h2, gfin_ref[...])
        o_ref[rs, :] = h2


def _ffn_ple(x, p, layer, wts, *, tile, final):
    n, d = x.shape
    dff = wts["w_up"].shape[2]
    pd = p.shape[-1]
    tile = min(tile, n)
    assert n % tile == 0
    return pl.pallas_call(
        functools.partial(_ffn_kernel, final=final),
        grid=(n // tile,),
        in_specs=[
            pl.BlockSpec((tile, d), lambda i: (i, 0)),
            pl.BlockSpec((None, tile, pd), lambda i: (layer, i, 0)),
            _layer_spec((1, d), layer),
            _layer_spec((d, dff), layer), _layer_spec((dff, d), layer),
            _layer_spec((1, d), layer),
            _layer_spec((d, d), layer), _layer_spec((pd, d), layer),
            _const_spec((1, d)),
        ],
        out_specs=pl.BlockSpec((tile, d), lambda i: (i, 0)),
        out_shape=jax.ShapeDtypeStruct(x.shape, x.dtype),
        scratch_shapes=[pltpu.VMEM((tile, dff), BF16)],
        compiler_params=pltpu.CompilerParams(dimension_semantics=("parallel",),
                                             vmem_limit_bytes=VMEM_LIMIT_BYTES),
        name="ffn_ple_final" if final else "ffn_ple",
    )(x, p, wts["g_mlp"], wts["w_up"], wts["w_down"], wts["g_ple"], wts["w_ple_gate"], wts["w_ple_proj"],
      wts["g_final"])


def _trunk(x, p, ret_states, conv_states, pos0, cos, sin, wts, *, nb, tt, conv_tt, ffn_tile):
    bsz, seq, d = x.shape
    depth = p.shape[0]
    p = p.reshape(depth, bsz * seq, p.shape[-1])
    decay = _decay_tables(min(tt, RET_CHUNK))
    new_ret = new_conv = None
    h = x
    for i in range(depth):
        r = i // 2
        if i % 2 == 0:
            h, new_ret = _ret_mixer(h, cos, sin, decay, pos0, wts, i, ret_states, new_ret, r, nb=nb, tt=tt)
        else:
            h, new_conv = _conv_mixer(h, wts, i, conv_states, new_conv, r, nb=nb, tt=conv_tt)
        h = _ffn_ple(h.reshape(bsz * seq, d), p, i, wts, tile=ffn_tile,
                     final=(i == depth - 1)).reshape(bsz, seq, d)
    return h, new_ret, new_conv


def kernel(x_prompt, x_sample, state_ret, state_conv, p_prompt, p_sample, g_mix, w_ret_in, ret_gn_g, ret_gn_b, w_ret_out, w_conv_in, conv_w, w_conv_out, g_mlp, w_up, w_down, g_ple, w_ple_gate, w_ple_proj, g_final):
    depth = p_prompt.shape[0]
    seq, dec_seq = x_prompt.shape[1], x_sample.shape[1]
    vw = w_ret_out.shape[1]
    qkw = (w_ret_in.shape[2] - 2 * vw) // 2
    dk = qkw // RET_HEADS

    rows = lambda a: a.reshape(a.shape[0], 1, a.shape[1])
    wts = {
        "g_mix": rows(g_mix), "g_mlp": rows(g_mlp), "g_ple": rows(g_ple), "g_final": g_final.reshape(1, -1),
        "ret_gn_g": rows(ret_gn_g), "ret_gn_b": rows(ret_gn_b), "conv_w": conv_w,
        "w_ret_in": w_ret_in.astype(BF16), "w_ret_out": w_ret_out.astype(BF16),
        "w_conv_in": w_conv_in.astype(BF16), "w_conv_out": w_conv_out.astype(BF16),
        "w_up": w_up.astype(BF16), "w_down": w_down.astype(BF16),
        "w_ple_gate": w_ple_gate.astype(BF16), "w_ple_proj": w_ple_proj.astype(BF16),
    }

    n_pos = max(seq, PAST_LEN + dec_seq)
    n_pos = -(-n_pos // ROPE_TILE) * ROPE_TILE
    cos, sin = _rope_tables(n_pos, dk)

    y_p, ret_p, conv_p = _trunk(x_prompt, p_prompt, None, None, 0, cos, sin, wts,
                                nb=1, tt=SEQ_TILE, conv_tt=CONV_TILE, ffn_tile=FFN_TILE)
    y_s, ret_s, conv_s = _trunk(x_sample, p_sample, state_ret, state_conv, PAST_LEN, cos, sin, wts,
                                nb=SAMPLE_STREAMS, tt=dec_seq, conv_tt=dec_seq, ffn_tile=FFN_TILE)
    return (y_p, y_s, ret_p, conv_p, ret_s, conv_s)
```

```python
import functools

import numpy as np
import jax
import jax.numpy as jnp
from jax import lax
from jax.experimental import pallas as pl
from jax.experimental.pallas import tpu as pltpu

F32 = jnp.float32
BF16 = jnp.bfloat16

RET_HEADS = 4
CONV_WIDTH = 3
ROPE_BASE = 10000.0
NORM_EPS = 1e-6
GN_EPS = 1e-6
PAST_LEN = 4096

LANES = 128
SUBLANES = 8
BF16_SUBLANES = 16
VMEM_LIMIT_BYTES = 56 * 1024 * 1024

SEQ_TILE = 512
CONV_TILE = 1024
SAMPLE_STREAMS = 4
FFN_TILE = 1024
FFN_TAIL_SPLIT = 4
RET_CHUNK = 256
ROPE_TILE = 512


def _const_spec(shape):
    nd = len(shape)
    return pl.BlockSpec(shape, lambda *_: (0,) * nd, pipeline_mode=pl.Buffered(1))


def _layer_spec(block, layer, col=0):
    return pl.BlockSpec((None,) + tuple(block), lambda *_: (layer, 0, col), pipeline_mode=pl.Buffered(1))


def _block_spec(block, col=0):
    return pl.BlockSpec(tuple(block), lambda *_: (0, col), pipeline_mode=pl.Buffered(1))


def _cast_specs(casts, n_steps, step_of):
    in_specs, out_specs, out_shapes = [], [], []
    for w, layer in casts:
        _, rows, cols = w.shape
        per = max(BF16_SUBLANES, rows // n_steps)
        assert rows % per == 0
        last = rows // per - 1
        in_specs.append(pl.BlockSpec((None, per, cols),
                                     lambda *ids, layer=layer, last=last: (layer, jnp.minimum(step_of(*ids), last), 0)))
        out_specs.append(pl.BlockSpec((per, cols),
                                      lambda *ids, last=last: (jnp.minimum(step_of(*ids), last), 0)))
        out_shapes.append(jax.ShapeDtypeStruct((rows, cols), BF16))
    return in_specs, out_specs, out_shapes


def _with_casts(body, n_in, n_out, n_cast):
    def kernel(*refs):
        ins, refs = refs[:n_in], refs[n_in:]
        cast_ins, refs = refs[:n_cast], refs[n_cast:]
        outs, refs = refs[:n_out], refs[n_out:]
        cast_outs, scratch = refs[:n_cast], refs[n_cast:]
        def cast_chunks():
            for src, dst in zip(cast_ins, cast_outs):
                dst[...] = src[...].astype(dst.dtype)

        body(*ins, *outs, *scratch, side_work=cast_chunks if n_cast else None)
    return kernel


def _rms(x, g):
    ms = jnp.mean(x * x, axis=-1, keepdims=True)
    return (x * lax.rsqrt(ms + NORM_EPS)) * g


def _dot(a, b):
    return jnp.dot(a, b, preferred_element_type=F32)


def _rope_table_kernel(inv_ref, sign_ref, cos_ref, sin_ref, cos_row_scr, sin_row_scr):
    rows = cos_ref.shape[0]
    tile = pl.program_id(0)

    @pl.when(tile == 0)
    def _rows_once():
        b = lax.broadcasted_iota(jnp.int32, cos_ref.shape, 0).astype(F32) * inv_ref[...]
        cos_row_scr[...] = jnp.cos(b)
        sin_row_scr[...] = jnp.sin(b)

    a = (tile * rows).astype(F32) * inv_ref[...]
    ca, sa = jnp.cos(a), jnp.sin(a)
    cb, sb = cos_row_scr[...], sin_row_scr[...]
    cos_ref[...] = ca * cb - sa * sb
    sin_ref[...] = (sa * cb + ca * sb) * sign_ref[...]


def _rope_tables(n_pos, dk):
    inv = 1.0 / (ROPE_BASE ** jnp.linspace(0.0, 1.0, dk // 2, dtype=F32))
    inv = jnp.repeat(inv, 2).reshape(1, dk)
    sign = jnp.tile(jnp.array([-1.0, 1.0], F32), dk // 2).reshape(1, dk)
    out = jax.ShapeDtypeStruct((n_pos, dk), F32)
    return pl.pallas_call(
        _rope_table_kernel,
        grid=(n_pos // ROPE_TILE,),
        in_specs=[pl.BlockSpec((1, dk), lambda i: (0, 0))] * 2,
        out_specs=[pl.BlockSpec((ROPE_TILE, dk), lambda i: (i, 0))] * 2,
        out_shape=[out, out],
        scratch_shapes=[pltpu.VMEM((ROPE_TILE, dk), F32)] * 2,
        compiler_params=pltpu.CompilerParams(dimension_semantics=("arbitrary",)),
        name="rope_tables",
    )(inv, sign)


def _log_gamma(h):
    return float(np.log(1.0 - 2.0 ** (-5.0 - h)))


def _decay_table_kernel(intra_ref, qdec_ref, kdec_ref):
    chunk = intra_ref.shape[1]
    diff = (lax.broadcasted_iota(jnp.int32, (chunk, chunk), 0)
            - lax.broadcasted_iota(jnp.int32, (chunk, chunk), 1))
    row = lax.broadcasted_iota(jnp.int32, (chunk, LANES), 0).astype(F32)
    for h in range(RET_HEADS):
        lg = _log_gamma(h)
        intra_ref[h] = jnp.where(diff >= 0, jnp.exp(jnp.maximum(diff, 0).astype(F32) * lg), 0.0)
        qdec_ref[h] = jnp.exp((row + 1.0) * lg)
        kdec_ref[h] = jnp.exp((chunk - 1.0 - row) * lg)


def _decay_tables(chunk):
    return pl.pallas_call(
        _decay_table_kernel,
        out_shape=[jax.ShapeDtypeStruct((RET_HEADS, chunk, chunk), F32),
                   jax.ShapeDtypeStruct((RET_HEADS, chunk, LANES), F32),
                   jax.ShapeDtypeStruct((RET_HEADS, chunk, LANES), F32)],
        name="decay_tables",
    )()


def _ret_kernel(*refs, nb, tt, chunk, has_state, has_prev, single_tile, side_work=None):
    refs = list(refs)
    (x_ref, cos_ref, sin_ref, intra_ref, qdec_ref, kdec_ref, gmix_ref, wq_ref, wk_ref, wv_ref, wg_ref,
     gng_ref, gnb_ref, wo_ref) = refs[:14]
    del refs[:14]
    s0_ref = refs.pop(0) if has_state else None
    if has_prev:
        refs.pop(0)
    o_ref, sout_ref, gated_scr = refs
    d = x_ref.shape[-1]
    dk = wq_ref.shape[1] // RET_HEADS
    dv = wv_ref.shape[1] // RET_HEADS
    t = pl.program_id(1)

    if not single_tile:
        @pl.when(t == 0)
        def _init_state():
            sout_ref[...] = s0_ref[...] if has_state else jnp.zeros_like(sout_ref)

    def state(b, h, first_chunk):
        if single_tile and first_chunk:
            return s0_ref[b, h] if has_state else jnp.zeros((dk, dv), F32)
        return sout_ref[b, h]

    even_lane = (lax.broadcasted_iota(jnp.int32, (nb * tt, LANES), 1) & 1) == 0
    groups = dk // LANES
    n_chunks = tt // chunk
    blocks = [(b, c) for b in range(nb) for c in range(n_chunks)]

    def rows(b, c):
        r0 = b * tt + c * chunk
        return slice(r0, r0 + chunk)

    def scale_rows(z, col):
        return jnp.concatenate([z[:, j * LANES:(j + 1) * LANES] * col for j in range(z.shape[1] // LANES)],
                               axis=1)

    x = x_ref[...].reshape(nb * tt, d)
    xn = _rms(x, gmix_ref[...]).astype(BF16)
    cos = cos_ref[...]
    sin = sin_ref[...]
    if nb > 1:
        cos = jnp.concatenate([cos] * nb, axis=0)
        sin = jnp.concatenate([sin] * nb, axis=0)

    def rotate(z):
        parts = []
        for j in range(z.shape[1] // LANES):
            zj = z[:, j * LANES:(j + 1) * LANES]
            tj = j % groups
            partner = jnp.where(even_lane, pltpu.roll(zj, LANES - 1, 1), pltpu.roll(zj, 1, 1))
            parts.append(zj * cos[:, tj * LANES:(tj + 1) * LANES]
                         + partner * sin[:, tj * LANES:(tj + 1) * LANES])
        return jnp.concatenate(parts, axis=1)

    q = rotate(_dot(xn, wq_ref[...])).astype(BF16)
    k = rotate(_dot(xn, wk_ref[...])) * (dk ** -0.5)
    v = _dot(xn, wv_ref[...]).astype(BF16)
    kb = k.astype(BF16)
    if side_work is not None:
        side_work()

    scores = {}
    for (b, c) in blocks:
        for h in range(RET_HEADS):
            sc = lax.dot_general(q[rows(b, c), h * dk:(h + 1) * dk], kb[rows(b, c), h * dk:(h + 1) * dk],
                                 (((1,), (1,)), ((), ())), preferred_element_type=F32)
            scores[b, c, h] = (sc * intra_ref[h]).astype(BF16)

    outs = {}
    for (b, c) in blocks:
        first = c == 0
        for h in range(RET_HEADS):
            qc = q[rows(b, c), h * dk:(h + 1) * dk]
            vc = v[rows(b, c), h * dv:(h + 1) * dv]
            outs[b, c, h] = (_dot(scores[b, c, h], vc)
                             + scale_rows(_dot(qc, state(b, h, first).astype(BF16)), qdec_ref[h]))
        for h in range(RET_HEADS):
            kd = scale_rows(k[rows(b, c), h * dk:(h + 1) * dk], kdec_ref[h]).astype(BF16)
            vc = v[rows(b, c), h * dv:(h + 1) * dv]
            sout_ref[b, h] = (state(b, h, first) * float(np.exp(chunk * _log_gamma(h)))
                              + lax.dot_general(kd, vc, (((0,), (0,)), ((), ())),
                                                preferred_element_type=F32))

    for h in range(RET_HEADS):
        g = _dot(xn, wg_ref[:, h * dv:(h + 1) * dv])
        gn_g = gng_ref[:, h * dv:(h + 1) * dv]
        gn_b = gnb_ref[:, h * dv:(h + 1) * dv]
        for b in range(nb):
            chunks = [outs[b, c, h] for c in range(n_chunks)]
            o = chunks[0] if n_chunks == 1 else jnp.concatenate(chunks, axis=0)
            mu = jnp.mean(o, axis=-1, keepdims=True)
            oc = o - mu
            var = jnp.mean(oc * oc, axis=-1, keepdims=True)
            on = (oc * lax.rsqrt(var + GN_EPS)) * gn_g + gn_b
            gb = g[b * tt:(b + 1) * tt]
            gated = (gb * jax.nn.sigmoid(gb)) * on
            gated_scr[b * tt:(b + 1) * tt, h * dv:(h + 1) * dv] = gated.astype(BF16)

    y = _dot(gated_scr[...], wo_ref[...])
    o_ref[...] = (x + y).reshape(nb, tt, d)


def _ret_mixer(x, cos, sin, decay, pos0, wts, mix_layer, states_in, states_out, layer, casts, *, nb, tt):
    bsz, seq, d = x.shape
    w_in, wo = wts["w_ret_in"][layer], wts["w_ret_out"][layer]
    n_layers, vw = len(wts["w_ret_in"]), wo.shape[0]
    qkw = (w_in.shape[1] - 2 * vw) // 2
    dk, dv = qkw // RET_HEADS, vw // RET_HEADS
    chunk = decay[0].shape[1]
    assert pos0 % tt == 0 and seq % tt == 0 and tt % chunk == 0 and bsz % nb == 0
    assert (2 * qkw) % vw == 0
    v_col = 2 * qkw // vw
    has_state = states_in is not None
    has_prev = states_out is not None
    pos_blk = pos0 // tt
    in_specs = [
        pl.BlockSpec((nb, tt, d), lambda b, t: (b, t, 0)),
        pl.BlockSpec((tt, dk), lambda b, t: (pos_blk + t, 0)),
        pl.BlockSpec((tt, dk), lambda b, t: (pos_blk + t, 0)),
        _const_spec((RET_HEADS, chunk, chunk)),
        _const_spec((RET_HEADS, chunk, LANES)), _const_spec((RET_HEADS, chunk, LANES)),
        _layer_spec((1, d), mix_layer),
        _block_spec((d, qkw), 0), _block_spec((d, qkw), 1),
        _block_spec((d, vw), v_col), _block_spec((d, vw), v_col + 1),
        _layer_spec((1, vw), layer), _layer_spec((1, vw), layer),
        _block_spec((vw, d)),
    ]
    args = [x, cos, sin, *decay, wts["g_mix"], w_in, w_in, w_in, w_in, wts["ret_gn_g"], wts["ret_gn_b"], wo]
    state_spec = pl.BlockSpec((None, nb, RET_HEADS, dk, dv), lambda b, t: (layer, b, 0, 0, 0))
    aliases = {}
    if has_state:
        in_specs.append(state_spec)
        args.append(states_in)
    if has_prev:
        aliases[len(args)] = 1
        in_specs.append(pl.BlockSpec(memory_space=pl.ANY))
        args.append(states_out)
    n_t = seq // tt
    c_in, c_out, c_shapes = _cast_specs(casts, (bsz // nb) * n_t, lambda b, t: b * n_t + t)
    body = functools.partial(_ret_kernel, nb=nb, tt=tt, chunk=chunk, has_state=has_state, has_prev=has_prev,
                             single_tile=(seq == tt))
    h, states, *cast = pl.pallas_call(
        _with_casts(body, len(args), 2, len(casts)),
        grid=(bsz // nb, n_t),
        in_specs=in_specs + c_in,
        out_specs=[pl.BlockSpec((nb, tt, d), lambda b, t: (b, t, 0)), state_spec] + c_out,
        out_shape=[jax.ShapeDtypeStruct(x.shape, x.dtype),
                   jax.ShapeDtypeStruct((n_layers, bsz, RET_HEADS, dk, dv), x.dtype)] + c_shapes,
        scratch_shapes=[pltpu.VMEM((nb * tt, vw), BF16)],
        input_output_aliases=aliases,
        compiler_params=pltpu.CompilerParams(dimension_semantics=("arbitrary", "arbitrary"),
                                             vmem_limit_bytes=VMEM_LIMIT_BYTES),
        name="ret_mixer_state" if has_state else "ret_mixer",
    )(*args, *[w for w, _ in casts])
    return h, states, cast


def _conv_kernel(*refs, nb, tt, has_state, has_prev, side_work=None):
    refs = list(refs)
    x_ref, gmix_ref, win_ref, cw_ref, wo_ref = refs[:5]
    del refs[:5]
    h0_ref = refs.pop(0) if has_state else None
    if has_prev:
        refs.pop(0)
    o_ref, hout_ref, u_scr = refs
    d = x_ref.shape[-1]
    nh = CONV_WIDTH - 1
    head = SUBLANES
    t = pl.program_id(1)

    @pl.when(t == 0)
    def _load_hist():
        if has_state:
            u_scr[:, head - nh:head, :] = h0_ref[...]
        else:
            u_scr[:, head - nh:head, :] = jnp.zeros((nb, nh, d), F32)

    x = x_ref[...].reshape(nb * tt, d)
    xn = _rms(x, gmix_ref[...]).astype(BF16)
    u = _dot(xn, win_ref[:, d:2 * d]) * _dot(xn, win_ref[:, 2 * d:3 * d])
    u_scr[:, head:head + tt, :] = u.reshape(nb, tt, d)
    bg = _dot(xn, win_ref[:, 0:d])
    if side_work is not None:
        side_work()
    y = cw_ref[CONV_WIDTH - 1:CONV_WIDTH, :] * u
    for j in range(CONV_WIDTH - 1):
        off = head - nh + j
        y = y + cw_ref[j:j + 1, :] * u_scr[:, off:off + tt, :].reshape(nb * tt, d)
    z = _dot((bg * y).astype(BF16), wo_ref[...])
    o_ref[...] = (x + z).reshape(nb, tt, d)
    tail = u_scr[:, head + tt - nh:head + tt, :]
    u_scr[:, head - nh:head, :] = tail
    hout_ref[...] = tail


def _conv_mixer(x, wts, mix_layer, hist_in, hist_out, layer, casts, *, nb, tt):
    bsz, seq, d = x.shape
    nh = CONV_WIDTH - 1
    n_layers = len(wts["w_conv_in"])
    assert seq % tt == 0 and bsz % nb == 0 and tt >= nh
    has_state = hist_in is not None
    has_prev = hist_out is not None
    in_specs = [
        pl.BlockSpec((nb, tt, d), lambda b, t: (b, t, 0)),
        _layer_spec((1, d), mix_layer),
        _block_spec((d, 3 * d)),
        _layer_spec((CONV_WIDTH, d), layer),
        _block_spec((d, d)),
    ]
    args = [x, wts["g_mix"], wts["w_conv_in"][layer], wts["conv_w"], wts["w_conv_out"][layer]]
    hist_spec = pl.BlockSpec((None, nb, nh, d), lambda b, t: (layer, b, 0, 0))
    aliases = {}
    if has_state:
        in_specs.append(hist_spec)
        args.append(hist_in)
    if has_prev:
        aliases[len(args)] = 1
        in_specs.append(pl.BlockSpec(memory_space=pl.ANY))
        args.append(hist_out)
    n_t = seq // tt
    c_in, c_out, c_shapes = _cast_specs(casts, (bsz // nb) * n_t, lambda b, t: b * n_t + t)
    body = functools.partial(_conv_kernel, nb=nb, tt=tt, has_state=has_state, has_prev=has_prev)
    h, hists, *cast = pl.pallas_call(
        _with_casts(body, len(args), 2, len(casts)),
        grid=(bsz // nb, n_t),
        in_specs=in_specs + c_in,
        out_specs=[pl.BlockSpec((nb, tt, d), lambda b, t: (b, t, 0)), hist_spec] + c_out,
        out_shape=[jax.ShapeDtypeStruct(x.shape, x.dtype),
                   jax.ShapeDtypeStruct((n_layers, bsz, nh, d), x.dtype)] + c_shapes,
        scratch_shapes=[pltpu.VMEM((nb, SUBLANES + tt, d), F32)],
        input_output_aliases=aliases,
        compiler_params=pltpu.CompilerParams(dimension_semantics=("arbitrary", "arbitrary"),
                                             vmem_limit_bytes=VMEM_LIMIT_BYTES),
        name="conv_mixer_state" if has_state else "conv_mixer",
    )(*args, *[w for w, _ in casts])
    return h, hists, cast


def _ffn_kernel(x_ref, p_ref, gmlp_ref, wup_ref, wdn_ref, gple_ref, wgate_ref, wproj_ref, gfin_ref,
                o_ref, u_scr, *, final, side_work=None):
    d = x_ref.shape[-1]
    dff = wup_ref.shape[1]
    x = x_ref[...]
    xn = _rms(x, gmlp_ref[...]).astype(BF16)
    for j in range(dff // d):
        u = jnp.maximum(_dot(xn, wup_ref[:, j * d:(j + 1) * d]), 0.0)
        u_scr[:, j * d:(j + 1) * d] = (u * u).astype(BF16)
    if side_work is not None:
        side_work()
    h1 = x + _dot(u_scr[...], wdn_ref[...])
    rows = x.shape[0] // FFN_TAIL_SPLIT
    for i in range(FFN_TAIL_SPLIT):
        rs = slice(i * rows, (i + 1) * rows)
        pp = _dot(p_ref[rs, :].astype(BF16), wproj_ref[...])
        gate = jax.nn.sigmoid(_dot(_rms(h1[rs], gple_ref[...]).astype(BF16), wgate_ref[...]))
        h2 = h1[rs] + pp * gate
        if final:
            h2 = _rms(h2, gfin_ref[...])
        o_ref[rs, :] = h2


def _ffn_ple(x, p, layer, wts, casts, *, tile, final):
    n, d = x.shape
    w_up, w_down = wts["w_up"][layer], wts["w_down"][layer]
    dff = w_up.shape[1]
    pd = p.shape[-1]
    tile = min(tile, n)
    assert n % tile == 0
    c_in, c_out, c_shapes = _cast_specs(casts, n // tile, lambda i: i)
    h, *cast = pl.pallas_call(
        _with_casts(functools.partial(_ffn_kernel, final=final), 9, 1, len(casts)),
        grid=(n // tile,),
        in_specs=[
            pl.BlockSpec((tile, d), lambda i: (i, 0)),
            pl.BlockSpec((None, tile, pd), lambda i: (layer, i, 0)),
            _layer_spec((1, d), layer),
            _block_spec((d, dff)), _block_spec((dff, d)),
            _layer_spec((1, d), layer),
            _block_spec((d, d)), _block_spec((pd, d)),
            _const_spec((1, d)),
        ] + c_in,
        out_specs=[pl.BlockSpec((tile, d), lambda i: (i, 0))] + c_out,
        out_shape=[jax.ShapeDtypeStruct(x.shape, x.dtype)] + c_shapes,
        scratch_shapes=[pltpu.VMEM((tile, dff), BF16)],
        compiler_params=pltpu.CompilerParams(dimension_semantics=("arbitrary",),
                                             vmem_limit_bytes=VMEM_LIMIT_BYTES),
        name="ffn_ple_final" if final else "ffn_ple",
    )(x, p, wts["g_mlp"], w_up, w_down, wts["g_ple"], wts["w_ple_gate"][layer], wts["w_ple_proj"][layer],
      wts["g_final"], *[w for w, _ in casts])
    return h, cast


_MIXER_WEIGHTS = (("w_ret_in", "w_ret_out"), ("w_conv_in", "w_conv_out"))
_FFN_WEIGHTS = ("w_up", "w_down", "w_ple_gate", "w_ple_proj")


def _trunk(x, p, ret_states, conv_states, pos0, cos, sin, wts, f32_weights, *, nb, tt, conv_tt, ffn_tile):
    bsz, seq, d = x.shape
    depth = p.shape[0]
    p = p.reshape(depth, bsz * seq, p.shape[-1])
    decay = _decay_tables(min(tt, RET_CHUNK))
    new_ret = new_conv = None
    h = x
    for i in range(depth):
        r = i // 2
        casts = [(f32_weights[name], i) for name in _FFN_WEIGHTS] if f32_weights else []
        if i % 2 == 0:
            h, new_ret, done = _ret_mixer(h, cos, sin, decay, pos0, wts, i, ret_states, new_ret, r, casts,
                                          nb=nb, tt=tt)
        else:
            h, new_conv, done = _conv_mixer(h, wts, i, conv_states, new_conv, r, casts, nb=nb, tt=conv_tt)
        for name, w in zip(_FFN_WEIGHTS, done):
            wts[name][i] = w
        nxt = i + 1
        names = _MIXER_WEIGHTS[nxt % 2] if f32_weights and nxt < depth else ()
        casts = [(f32_weights[name], nxt // 2) for name in names]
        h, done = _ffn_ple(h.reshape(bsz * seq, d), p, i, wts, casts, tile=ffn_tile, final=(i == depth - 1))
        h = h.reshape(bsz, seq, d)
        for name, w in zip(names, done):
            wts[name][nxt // 2] = w
    return h, new_ret, new_conv


def kernel(x_prompt, x_sample, state_ret, state_conv, p_prompt, p_sample, g_mix, w_ret_in, ret_gn_g, ret_gn_b, w_ret_out, w_conv_in, conv_w, w_conv_out, g_mlp, w_up, w_down, g_ple, w_ple_gate, w_ple_proj, g_final):
    depth = p_prompt.shape[0]
    seq, dec_seq = x_prompt.shape[1], x_sample.shape[1]
    vw = w_ret_out.shape[1]
    qkw = (w_ret_in.shape[2] - 2 * vw) // 2
    dk = qkw // RET_HEADS

    rows = lambda a: a.reshape(a.shape[0], 1, a.shape[1])
    f32_weights = {"w_ret_in": w_ret_in, "w_ret_out": w_ret_out, "w_conv_in": w_conv_in,
                   "w_conv_out": w_conv_out, "w_up": w_up, "w_down": w_down,
                   "w_ple_gate": w_ple_gate, "w_ple_proj": w_ple_proj}
    wts = {
        "g_mix": rows(g_mix), "g_mlp": rows(g_mlp), "g_ple": rows(g_ple), "g_final": g_final.reshape(1, -1),
        "ret_gn_g": rows(ret_gn_g), "ret_gn_b": rows(ret_gn_b), "conv_w": conv_w,
    }
    for name, w in f32_weights.items():
        wts[name] = [None] * w.shape[0]
    for name in _MIXER_WEIGHTS[0]:
        wts[name][0] = f32_weights[name][0].astype(BF16)

    n_pos = max(seq, PAST_LEN + dec_seq)
    n_pos = -(-n_pos // ROPE_TILE) * ROPE_TILE
    cos, sin = _rope_tables(n_pos, dk)

    y_p, ret_p, conv_p = _trunk(x_prompt, p_prompt, None, None, 0, cos, sin, wts, f32_weights,
                                nb=1, tt=SEQ_TILE, conv_tt=CONV_TILE, ffn_tile=FFN_TILE)
    y_s, ret_s, conv_s = _trunk(x_sample, p_sample, state_ret, state_conv, PAST_LEN, cos, sin, wts, None,
                                nb=SAMPLE_STREAMS, tt=dec_seq, conv_tt=dec_seq, ffn_tile=FFN_TILE)
    return (y_p, y_s, ret_p, conv_p, ret_s, conv_s)
```

```python
import functools

import numpy as np
import jax
import jax.numpy as jnp
from jax import lax
from jax.experimental import pallas as pl
from jax.experimental.pallas import tpu as pltpu

F32 = jnp.float32
BF16 = jnp.bfloat16

RET_HEADS = 4
CONV_WIDTH = 3
ROPE_BASE = 10000.0
NORM_EPS = 1e-6
GN_EPS = 1e-6
PAST_LEN = 4096

LANES = 128
SUBLANES = 8
BF16_SUBLANES = 16
VMEM_LIMIT_BYTES = 56 * 1024 * 1024

SEQ_TILE = 512
CONV_TILE = 1024
RET_SAMPLE_STREAMS = 2
CONV_SAMPLE_STREAMS = 4
FFN_TILE = 1024
FFN_TAIL_SPLIT = 4
RET_CHUNK = 256
ROPE_TILE = 512


def _const_spec(shape):
    nd = len(shape)
    return pl.BlockSpec(shape, lambda *_: (0,) * nd, pipeline_mode=pl.Buffered(1))


def _layer_spec(block, layer, col=0):
    return pl.BlockSpec((None,) + tuple(block), lambda *_: (layer, 0, col), pipeline_mode=pl.Buffered(1))


def _block_spec(block, col=0):
    return pl.BlockSpec(tuple(block), lambda *_: (0, col), pipeline_mode=pl.Buffered(1))


def _cast_specs(casts, n_steps, step_of):
    in_specs, out_specs, out_shapes = [], [], []
    for w, layer in casts:
        _, rows, cols = w.shape
        n_chunks = max(c for c in range(1, n_steps + 1)
                       if rows % c == 0 and (rows // c) % BF16_SUBLANES == 0)
        per = rows // n_chunks
        last = n_chunks - 1
        in_specs.append(pl.BlockSpec((None, per, cols),
                                     lambda *ids, layer=layer, last=last: (layer, jnp.minimum(step_of(*ids), last), 0)))
        out_specs.append(pl.BlockSpec((per, cols),
                                      lambda *ids, last=last: (jnp.minimum(step_of(*ids), last), 0)))
        out_shapes.append(jax.ShapeDtypeStruct((rows, cols), BF16))
    return in_specs, out_specs, out_shapes


def _with_casts(body, n_in, n_out, n_cast):
    def kernel(*refs):
        ins, refs = refs[:n_in], refs[n_in:]
        cast_ins, refs = refs[:n_cast], refs[n_cast:]
        outs, refs = refs[:n_out], refs[n_out:]
        cast_outs, scratch = refs[:n_cast], refs[n_cast:]
        def cast_chunks():
            for src, dst in zip(cast_ins, cast_outs):
                dst[...] = src[...].astype(dst.dtype)

        body(*ins, *outs, *scratch, side_work=cast_chunks if n_cast else None)
    return kernel


def _rms(x, g):
    ms = jnp.mean(x * x, axis=-1, keepdims=True)
    return (x * lax.rsqrt(ms + NORM_EPS)) * g


def _dot(a, b):
    return jnp.dot(a, b, preferred_element_type=F32)


def _rope_table_kernel(inv_ref, sign_ref, cos_ref, sin_ref, cos_row_scr, sin_row_scr, side_work=None):
    if side_work is not None:
        side_work()
    rows = cos_ref.shape[0]
    tile = pl.program_id(0)

    @pl.when(tile == 0)
    def _rows_once():
        b = lax.broadcasted_iota(jnp.int32, cos_ref.shape, 0).astype(F32) * inv_ref[...]
        cos_row_scr[...] = jnp.cos(b)
        sin_row_scr[...] = jnp.sin(b)

    a = (tile * rows).astype(F32) * inv_ref[...]
    ca, sa = jnp.cos(a), jnp.sin(a)
    cb, sb = cos_row_scr[...], sin_row_scr[...]
    cos_ref[...] = ca * cb - sa * sb
    sin_ref[...] = (sa * cb + ca * sb) * sign_ref[...]


def _rope_tables(n_pos, dk, casts):
    inv = 1.0 / (ROPE_BASE ** jnp.linspace(0.0, 1.0, dk // 2, dtype=F32))
    inv = jnp.repeat(inv, 2).reshape(1, dk)
    sign = jnp.tile(jnp.array([-1.0, 1.0], F32), dk // 2).reshape(1, dk)
    out = jax.ShapeDtypeStruct((n_pos, dk), F32)
    n_steps = n_pos // ROPE_TILE
    c_in, c_out, c_shapes = _cast_specs(casts, n_steps, lambda i: i)
    cos, sin, *cast = pl.pallas_call(
        _with_casts(_rope_table_kernel, 2, 2, len(casts)),
        grid=(n_steps,),
        in_specs=[pl.BlockSpec((1, dk), lambda i: (0, 0))] * 2 + c_in,
        out_specs=[pl.BlockSpec((ROPE_TILE, dk), lambda i: (i, 0))] * 2 + c_out,
        out_shape=[out, out] + c_shapes,
        scratch_shapes=[pltpu.VMEM((ROPE_TILE, dk), F32)] * 2,
        compiler_params=pltpu.CompilerParams(dimension_semantics=("arbitrary",)),
        name="rope_tables",
    )(inv, sign, *[w for w, _ in casts])
    return cos, sin, cast


def _log_gamma(h):
    return float(np.log(1.0 - 2.0 ** (-5.0 - h)))


def _decay_table_kernel(intra_ref, qdec_ref, kdec_ref):
    chunk = intra_ref.shape[1]
    diff = (lax.broadcasted_iota(jnp.int32, (chunk, chunk), 0)
            - lax.broadcasted_iota(jnp.int32, (chunk, chunk), 1))
    row = lax.broadcasted_iota(jnp.int32, (chunk, LANES), 0).astype(F32)
    for h in range(RET_HEADS):
        lg = _log_gamma(h)
        intra_ref[h] = jnp.where(diff >= 0, jnp.exp(jnp.maximum(diff, 0).astype(F32) * lg), 0.0)
        qdec_ref[h] = jnp.exp((row + 1.0) * lg)
        kdec_ref[h] = jnp.exp((chunk - 1.0 - row) * lg)


def _decay_tables(chunk):
    return pl.pallas_call(
        _decay_table_kernel,
        out_shape=[jax.ShapeDtypeStruct((RET_HEADS, chunk, chunk), F32),
                   jax.ShapeDtypeStruct((RET_HEADS, chunk, LANES), F32),
                   jax.ShapeDtypeStruct((RET_HEADS, chunk, LANES), F32)],
        name="decay_tables",
    )()


def _ret_kernel(*refs, nb, tt, chunk, layer, has_state, has_prev, single_tile, side_work=None):
    refs = list(refs)
    (x_ref, cos_ref, sin_ref, intra_ref, qdec_ref, kdec_ref, gmix_ref, wq_ref, wk_ref, wv_ref, wg_ref,
     gng_ref, gnb_ref, wo_ref) = refs[:14]
    del refs[:14]
    s0_ref = refs.pop(0) if has_state else None
    if has_prev:
        refs.pop(0)
    o_ref, states_ref, gated_scr = refs
    d = x_ref.shape[-1]
    dk = wq_ref.shape[1] // RET_HEADS
    dv = wv_ref.shape[1] // RET_HEADS
    t = pl.program_id(1)

    if has_prev:
        sout_ref = states_ref
    else:
        sout_ref = states_ref.at[layer]

        def _fill_other_slots():
            for other in range(states_ref.shape[0]):
                if other != layer:
                    states_ref[other] = jnp.zeros(states_ref.shape[1:], F32)

        if single_tile:
            _fill_other_slots()
        else:
            pl.when(t == 0)(_fill_other_slots)

    if not single_tile:
        @pl.when(t == 0)
        def _init_state():
            sout_ref[...] = s0_ref[...] if has_state else jnp.zeros(sout_ref.shape, F32)

    def state(b, h, first_chunk):
        if single_tile and first_chunk:
            return s0_ref[b, h] if has_state else jnp.zeros((dk, dv), F32)
        return sout_ref[b, h]

    even_lane = (lax.broadcasted_iota(jnp.int32, (nb * tt, LANES), 1) & 1) == 0
    groups = dk // LANES
    n_chunks = tt // chunk
    blocks = [(b, c) for b in range(nb) for c in range(n_chunks)]

    def rows(b, c):
        r0 = b * tt + c * chunk
        return slice(r0, r0 + chunk)

    def scale_rows(z, col):
        return jnp.concatenate([z[:, j * LANES:(j + 1) * LANES] * col for j in range(z.shape[1] // LANES)],
                               axis=1)

    x = x_ref[...].reshape(nb * tt, d)
    xn = _rms(x, gmix_ref[...]).astype(BF16)
    cos = cos_ref[...]
    sin = sin_ref[...]
    if nb > 1:
        cos = jnp.concatenate([cos] * nb, axis=0)
        sin = jnp.concatenate([sin] * nb, axis=0)

    def rotate(z):
        parts = []
        for j in range(z.shape[1] // LANES):
            zj = z[:, j * LANES:(j + 1) * LANES]
            tj = j % groups
            partner = jnp.where(even_lane, pltpu.roll(zj, LANES - 1, 1), pltpu.roll(zj, 1, 1))
            parts.append(zj * cos[:, tj * LANES:(tj + 1) * LANES]
                         + partner * sin[:, tj * LANES:(tj + 1) * LANES])
        return jnp.concatenate(parts, axis=1)

    q = rotate(_dot(xn, wq_ref[...])).astype(BF16)
    k = rotate(_dot(xn, wk_ref[...])) * (dk ** -0.5)
    v = _dot(xn, wv_ref[...]).astype(BF16)
    kb = k.astype(BF16)
    if side_work is not None:
        side_work()

    scores = {}
    for (b, c) in blocks:
        for h in range(RET_HEADS):
            sc = lax.dot_general(q[rows(b, c), h * dk:(h + 1) * dk], kb[rows(b, c), h * dk:(h + 1) * dk],
                                 (((1,), (1,)), ((), ())), preferred_element_type=F32)
            scores[b, c, h] = (sc * intra_ref[h]).astype(BF16)

    outs = {}
    for (b, c) in blocks:
        first = c == 0
        for h in range(RET_HEADS):
            qc = q[rows(b, c), h * dk:(h + 1) * dk]
            vc = v[rows(b, c), h * dv:(h + 1) * dv]
            outs[b, c, h] = (_dot(scores[b, c, h], vc)
                             + scale_rows(_dot(qc, state(b, h, first).astype(BF16)), qdec_ref[h]))
        for h in range(RET_HEADS):
            kd = scale_rows(k[rows(b, c), h * dk:(h + 1) * dk], kdec_ref[h]).astype(BF16)
            vc = v[rows(b, c), h * dv:(h + 1) * dv]
            sout_ref[b, h] = (state(b, h, first) * float(np.exp(chunk * _log_gamma(h)))
                              + lax.dot_general(kd, vc, (((0,), (0,)), ((), ())),
                                                preferred_element_type=F32))

    for h in range(RET_HEADS):
        g = _dot(xn, wg_ref[:, h * dv:(h + 1) * dv])
        gn_g = gng_ref[:, h * dv:(h + 1) * dv]
        gn_b = gnb_ref[:, h * dv:(h + 1) * dv]
        for b in range(nb):
            chunks = [outs[b, c, h] for c in range(n_chunks)]
            o = chunks[0] if n_chunks == 1 else jnp.concatenate(chunks, axis=0)
            mu = jnp.mean(o, axis=-1, keepdims=True)
            oc = o - mu
            var = jnp.mean(oc * oc, axis=-1, keepdims=True)
            on = (oc * lax.rsqrt(var + GN_EPS)) * gn_g + gn_b
            gb = g[b * tt:(b + 1) * tt]
            gated = (gb * jax.nn.sigmoid(gb)) * on
            gated_scr[b * tt:(b + 1) * tt, h * dv:(h + 1) * dv] = gated.astype(BF16)

    y = _dot(gated_scr[...], wo_ref[...])
    o_ref[...] = (x + y).reshape(nb, tt, d)


def _ret_mixer(x, cos, sin, decay, pos0, wts, mix_layer, states_in, states_out, layer, casts, *, nb, tt):
    bsz, seq, d = x.shape
    w_in, wo = wts["w_ret_in"][layer], wts["w_ret_out"][layer]
    n_layers, vw = len(wts["w_ret_in"]), wo.shape[0]
    qkw = (w_in.shape[1] - 2 * vw) // 2
    dk, dv = qkw // RET_HEADS, vw // RET_HEADS
    chunk = decay[0].shape[1]
    assert pos0 % tt == 0 and seq % tt == 0 and tt % chunk == 0 and bsz % nb == 0
    assert (2 * qkw) % vw == 0
    v_col = 2 * qkw // vw
    has_state = states_in is not None
    has_prev = states_out is not None
    pos_blk = pos0 // tt
    in_specs = [
        pl.BlockSpec((nb, tt, d), lambda b, t: (b, t, 0)),
        pl.BlockSpec((tt, dk), lambda b, t: (pos_blk + t, 0)),
        pl.BlockSpec((tt, dk), lambda b, t: (pos_blk + t, 0)),
        _const_spec((RET_HEADS, chunk, chunk)),
        _const_spec((RET_HEADS, chunk, LANES)), _const_spec((RET_HEADS, chunk, LANES)),
        _layer_spec((1, d), mix_layer),
        _block_spec((d, qkw), 0), _block_spec((d, qkw), 1),
        _block_spec((d, vw), v_col), _block_spec((d, vw), v_col + 1),
        _layer_spec((1, vw), layer), _layer_spec((1, vw), layer),
        _block_spec((vw, d)),
    ]
    args = [x, cos, sin, *decay, wts["g_mix"], w_in, w_in, w_in, w_in, wts["ret_gn_g"], wts["ret_gn_b"], wo]
    state_spec = pl.BlockSpec((None, nb, RET_HEADS, dk, dv), lambda b, t: (layer, b, 0, 0, 0))
    aliases = {}
    if has_state:
        in_specs.append(state_spec)
        args.append(states_in)
    if has_prev:
        aliases[len(args)] = 1
        in_specs.append(pl.BlockSpec(memory_space=pl.ANY))
        args.append(states_out)
        out_state_spec = state_spec
    else:
        out_state_spec = pl.BlockSpec((n_layers, nb, RET_HEADS, dk, dv), lambda b, t: (0, b, 0, 0, 0))
    n_t = seq // tt
    c_in, c_out, c_shapes = _cast_specs(casts, (bsz // nb) * n_t, lambda b, t: b * n_t + t)
    body = functools.partial(_ret_kernel, nb=nb, tt=tt, chunk=chunk, layer=layer, has_state=has_state,
                             has_prev=has_prev, single_tile=(seq == tt))
    h, states, *cast = pl.pallas_call(
        _with_casts(body, len(args), 2, len(casts)),
        grid=(bsz // nb, n_t),
        in_specs=in_specs + c_in,
        out_specs=[pl.BlockSpec((nb, tt, d), lambda b, t: (b, t, 0)), out_state_spec] + c_out,
        out_shape=[jax.ShapeDtypeStruct(x.shape, x.dtype),
                   jax.ShapeDtypeStruct((n_layers, bsz, RET_HEADS, dk, dv), x.dtype)] + c_shapes,
        scratch_shapes=[pltpu.VMEM((nb * tt, vw), BF16)],
        input_output_aliases=aliases,
        compiler_params=pltpu.CompilerParams(dimension_semantics=("arbitrary", "arbitrary"),
                                             vmem_limit_bytes=VMEM_LIMIT_BYTES),
        name="ret_mixer_state" if has_state else "ret_mixer",
    )(*args, *[w for w, _ in casts])
    return h, states, cast


def _conv_kernel(*refs, nb, tt, layer, has_state, has_prev, side_work=None):
    refs = list(refs)
    x_ref, gmix_ref, win_ref, cw_ref, wo_ref = refs[:5]
    del refs[:5]
    h0_ref = refs.pop(0) if has_state else None
    if has_prev:
        refs.pop(0)
    o_ref, hists_ref, u_scr = refs
    d = x_ref.shape[-1]
    nh = CONV_WIDTH - 1
    if has_prev:
        hout_ref = hists_ref
    else:
        hout_ref = hists_ref.at[layer]
        for other in range(hists_ref.shape[0]):
            if other != layer:
                hists_ref[other] = jnp.zeros(hists_ref.shape[1:], F32)
    head = SUBLANES
    t = pl.program_id(1)

    @pl.when(t == 0)
    def _load_hist():
        if has_state:
            u_scr[:, head - nh:head, :] = h0_ref[...]
        else:
            u_scr[:, head - nh:head, :] = jnp.zeros((nb, nh, d), F32)

    x = x_ref[...].reshape(nb * tt, d)
    xn = _rms(x, gmix_ref[...]).astype(BF16)
    u = _dot(xn, win_ref[:, d:2 * d]) * _dot(xn, win_ref[:, 2 * d:3 * d])
    u_scr[:, head:head + tt, :] = u.reshape(nb, tt, d)
    bg = _dot(xn, win_ref[:, 0:d])
    if side_work is not None:
        side_work()
    y = cw_ref[CONV_WIDTH - 1:CONV_WIDTH, :] * u
    for j in range(CONV_WIDTH - 1):
        off = head - nh + j
        y = y + cw_ref[j:j + 1, :] * u_scr[:, off:off + tt, :].reshape(nb * tt, d)
    z = _dot((bg * y).astype(BF16), wo_ref[...])
    o_ref[...] = (x + z).reshape(nb, tt, d)
    tail = u_scr[:, head + tt - nh:head + tt, :]
    u_scr[:, head - nh:head, :] = tail
    hout_ref[...] = tail


def _conv_mixer(x, wts, mix_layer, hist_in, hist_out, layer, casts, *, nb, tt):
    bsz, seq, d = x.shape
    nh = CONV_WIDTH - 1
    n_layers = len(wts["w_conv_in"])
    assert seq % tt == 0 and bsz % nb == 0 and tt >= nh
    has_state = hist_in is not None
    has_prev = hist_out is not None
    in_specs = [
        pl.BlockSpec((nb, tt, d), lambda b, t: (b, t, 0)),
        _layer_spec((1, d), mix_layer),
        _block_spec((d, 3 * d)),
        _layer_spec((CONV_WIDTH, d), layer),
        _block_spec((d, d)),
    ]
    args = [x, wts["g_mix"], wts["w_conv_in"][layer], wts["conv_w"], wts["w_conv_out"][layer]]
    hist_spec = pl.BlockSpec((None, nb, nh, d), lambda b, t: (layer, b, 0, 0))
    aliases = {}
    if has_state:
        in_specs.append(hist_spec)
        args.append(hist_in)
    if has_prev:
        aliases[len(args)] = 1
        in_specs.append(pl.BlockSpec(memory_space=pl.ANY))
        args.append(hist_out)
        out_hist_spec = hist_spec
    else:
        out_hist_spec = pl.BlockSpec((n_layers, nb, nh, d), lambda b, t: (0, b, 0, 0))
    n_t = seq // tt
    c_in, c_out, c_shapes = _cast_specs(casts, (bsz // nb) * n_t, lambda b, t: b * n_t + t)
    body = functools.partial(_conv_kernel, nb=nb, tt=tt, layer=layer, has_state=has_state, has_prev=has_prev)
    h, hists, *cast = pl.pallas_call(
        _with_casts(body, len(args), 2, len(casts)),
        grid=(bsz // nb, n_t),
        in_specs=in_specs + c_in,
        out_specs=[pl.BlockSpec((nb, tt, d), lambda b, t: (b, t, 0)), out_hist_spec] + c_out,
        out_shape=[jax.ShapeDtypeStruct(x.shape, x.dtype),
                   jax.ShapeDtypeStruct((n_layers, bsz, nh, d), x.dtype)] + c_shapes,
        scratch_shapes=[pltpu.VMEM((nb, SUBLANES + tt, d), F32)],
        input_output_aliases=aliases,
        compiler_params=pltpu.CompilerParams(dimension_semantics=("arbitrary", "arbitrary"),
                                             vmem_limit_bytes=VMEM_LIMIT_BYTES),
        name="conv_mixer_state" if has_state else "conv_mixer",
    )(*args, *[w for w, _ in casts])
    return h, hists, cast


def _ffn_kernel(x_ref, p_ref, gmlp_ref, wup_ref, wdn_ref, gple_ref, wgate_ref, wproj_ref, gfin_ref,
                o_ref, u_scr, *, final, side_work=None):
    d = x_ref.shape[-1]
    dff = wup_ref.shape[1]
    x = x_ref[...]
    xn = _rms(x, gmlp_ref[...]).astype(BF16)
    for j in range(dff // d):
        u = jnp.maximum(_dot(xn, wup_ref[:, j * d:(j + 1) * d]), 0.0)
        u_scr[:, j * d:(j + 1) * d] = (u * u).astype(BF16)
    if side_work is not None:
        side_work()
    h1 = x + _dot(u_scr[...], wdn_ref[...])
    rows = x.shape[0] // FFN_TAIL_SPLIT
    for i in range(FFN_TAIL_SPLIT):
        rs = slice(i * rows, (i + 1) * rows)
        pp = _dot(p_ref[rs, :].astype(BF16), wproj_ref[...])
        gate = jax.nn.sigmoid(_dot(_rms(h1[rs], gple_ref[...]).astype(BF16), wgate_ref[...]))
        h2 = h1[rs] + pp * gate
        if final:
            h2 = _rms(h2, gfin_ref[...])
        o_ref[rs, :] = h2


def _ffn_ple(x, p, layer, wts, casts, *, tile, final):
    n, d = x.shape
    w_up, w_down = wts["w_up"][layer], wts["w_down"][layer]
    dff = w_up.shape[1]
    pd = p.shape[-1]
    tile = min(tile, n)
    assert n % tile == 0
    c_in, c_out, c_shapes = _cast_specs(casts, n // tile, lambda i: i)
    h, *cast = pl.pallas_call(
        _with_casts(functools.partial(_ffn_kernel, final=final), 9, 1, len(casts)),
        grid=(n // tile,),
        in_specs=[
            pl.BlockSpec((tile, d), lambda i: (i, 0)),
            pl.BlockSpec((None, tile, pd), lambda i: (layer, i, 0)),
            _layer_spec((1, d), layer),
            _block_spec((d, dff)), _block_spec((dff, d)),
            _layer_spec((1, d), layer),
            _block_spec((d, d)), _block_spec((pd, d)),
            _const_spec((1, d)),
        ] + c_in,
        out_specs=[pl.BlockSpec((tile, d), lambda i: (i, 0))] + c_out,
        out_shape=[jax.ShapeDtypeStruct(x.shape, x.dtype)] + c_shapes,
        scratch_shapes=[pltpu.VMEM((tile, dff), BF16)],
        compiler_params=pltpu.CompilerParams(dimension_semantics=("arbitrary",),
                                             vmem_limit_bytes=VMEM_LIMIT_BYTES),
        name="ffn_ple_final" if final else "ffn_ple",
    )(x, p, wts["g_mlp"], w_up, w_down, wts["g_ple"], wts["w_ple_gate"][layer], wts["w_ple_proj"][layer],
      wts["g_final"], *[w for w, _ in casts])
    return h, cast


_MIXER_WEIGHTS = (("w_ret_in", "w_ret_out"), ("w_conv_in", "w_conv_out"))
_FFN_WEIGHTS = ("w_up", "w_down", "w_ple_gate", "w_ple_proj")


def _trunk(x, p, ret_states, conv_states, pos0, cos, sin, wts, f32_weights, *, nb, conv_nb, tt, conv_tt,
           ffn_tile):
    bsz, seq, d = x.shape
    depth = p.shape[0]
    p = p.reshape(depth, bsz * seq, p.shape[-1])
    decay = _decay_tables(min(tt, RET_CHUNK))
    new_ret = new_conv = None
    h = x
    for i in range(depth):
        r = i // 2
        casts = [(f32_weights[name], i) for name in _FFN_WEIGHTS] if f32_weights else []
        if i % 2 == 0:
            h, new_ret, done = _ret_mixer(h, cos, sin, decay, pos0, wts, i, ret_states, new_ret, r, casts,
                                          nb=nb, tt=tt)
        else:
            h, new_conv, done = _conv_mixer(h, wts, i, conv_states, new_conv, r, casts, nb=conv_nb,
                                            tt=conv_tt)
        for name, w in zip(_FFN_WEIGHTS, done):
            wts[name][i] = w
        nxt = i + 1
        names = _MIXER_WEIGHTS[nxt % 2] if f32_weights and nxt < depth else ()
        casts = [(f32_weights[name], nxt // 2) for name in names]
        h, done = _ffn_ple(h.reshape(bsz * seq, d), p, i, wts, casts, tile=ffn_tile, final=(i == depth - 1))
        h = h.reshape(bsz, seq, d)
        for name, w in zip(names, done):
            wts[name][nxt // 2] = w
    return h, new_ret, new_conv


def kernel(x_prompt, x_sample, state_ret, state_conv, p_prompt, p_sample, g_mix, w_ret_in, ret_gn_g, ret_gn_b, w_ret_out, w_conv_in, conv_w, w_conv_out, g_mlp, w_up, w_down, g_ple, w_ple_gate, w_ple_proj, g_final):
    depth = p_prompt.shape[0]
    seq, dec_seq = x_prompt.shape[1], x_sample.shape[1]
    vw = w_ret_out.shape[1]
    qkw = (w_ret_in.shape[2] - 2 * vw) // 2
    dk = qkw // RET_HEADS

    rows = lambda a: a.reshape(a.shape[0], 1, a.shape[1])
    f32_weights = {"w_ret_in": w_ret_in, "w_ret_out": w_ret_out, "w_conv_in": w_conv_in,
                   "w_conv_out": w_conv_out, "w_up": w_up, "w_down": w_down,
                   "w_ple_gate": w_ple_gate, "w_ple_proj": w_ple_proj}
    wts = {
        "g_mix": rows(g_mix), "g_mlp": rows(g_mlp), "g_ple": rows(g_ple), "g_final": g_final.reshape(1, -1),
        "ret_gn_g": rows(ret_gn_g), "ret_gn_b": rows(ret_gn_b), "conv_w": conv_w,
    }
    for name, w in f32_weights.items():
        wts[name] = [None] * w.shape[0]
    n_pos = max(seq, PAST_LEN + dec_seq)
    n_pos = -(-n_pos // ROPE_TILE) * ROPE_TILE
    cos, sin, first = _rope_tables(n_pos, dk, [(f32_weights[name], 0) for name in _MIXER_WEIGHTS[0]])
    for name, w in zip(_MIXER_WEIGHTS[0], first):
        wts[name][0] = w

    y_p, ret_p, conv_p = _trunk(x_prompt, p_prompt, None, None, 0, cos, sin, wts, f32_weights,
                                nb=1, conv_nb=1, tt=SEQ_TILE, conv_tt=CONV_TILE, ffn_tile=FFN_TILE)
    y_s, ret_s, conv_s = _trunk(x_sample, p_sample, state_ret, state_conv, PAST_LEN, cos, sin, wts, None,
                                nb=RET_SAMPLE_STREAMS, conv_nb=CONV_SAMPLE_STREAMS, tt=dec_seq, conv_tt=dec_seq,
                                ffn_tile=FFN_TILE)
    return (y_p, y_s, ret_p, conv_p, ret_s, conv_s)
```
